```python
import math, functools
import jax, jax.numpy as jnp
from jax import lax
import numpy as np

D_MODEL = 1024
BATCH = 32
SEQ = 2048
DEPTH = 2
DEC_BATCH = 8
DEC_SEQ = 2048
PAST_LEN = 128

N_MIXERS = 2
N_FOURIER_LAYERS = (DEPTH + 1) // 2
N_MLA_LAYERS = DEPTH // 2
N_FOURIER_GROUPS = 4
FOURIER_GROUP_DIM = D_MODEL // N_FOURIER_GROUPS
N_HEADS = 16
QK_NOPE_DIM = 64
QK_ROPE_DIM = 32
V_HEAD_DIM = 64
Q_LORA_RANK = D_MODEL // 4
KV_LORA_RANK = D_MODEL // 8
MLA_IN_DIM = Q_LORA_RANK + KV_LORA_RANK + QK_ROPE_DIM
ATTN_SCALE = (QK_NOPE_DIM + QK_ROPE_DIM) ** -0.5
ROPE_THETA = 10000.0
Q_BLOCK = 128
D_FF = ((8 * D_MODEL // 3 + 127) // 128) * 128
CONV_WIDTH = 3
EPS = 1e-6

kernel_name = "hybrid_fnet_mla_convffn_encoder"


def _rmsnorm(x, g):
    xf = x.astype(jnp.float32)
    y = xf * lax.rsqrt(jnp.mean(xf * xf, axis=-1, keepdims=True) + EPS)
    return (y * g.astype(jnp.float32)).astype(x.dtype)


def _fourier_mix(h, w_out):
    B, S, D = h.shape
    hg = h.astype(jnp.float32).reshape(B, S, N_FOURIER_GROUPS, FOURIER_GROUP_DIM)
    f = jnp.fft.fftn(hg, axes=(1, 3), norm="ortho").real
    return f.reshape(B, S, D).astype(h.dtype) @ w_out


def _rope_tables(S):
    inv = 1.0 / (ROPE_THETA ** (jnp.arange(0, QK_ROPE_DIM, 2, dtype=jnp.float32) / QK_ROPE_DIM))
    ang = jnp.arange(S, dtype=jnp.float32)[:, None] * inv[None, :]
    return jnp.cos(ang), jnp.sin(ang)


def _apply_rope(x, cos, sin):
    xf = x.astype(jnp.float32)
    half = QK_ROPE_DIM // 2
    x1, x2 = xf[..., :half], xf[..., half:]
    return jnp.concatenate([x1 * cos - x2 * sin, x2 * cos + x1 * sin], axis=-1).astype(x.dtype)


def _mla(h, w_in, g_q, g_kv, w_uq, w_ukv, w_o, cos, sin):
    B, S, _ = h.shape
    a = h @ w_in
    c_q = _rmsnorm(a[..., :Q_LORA_RANK], g_q)
    c_kv = _rmsnorm(a[..., Q_LORA_RANK:Q_LORA_RANK + KV_LORA_RANK], g_kv)
    k_rope = _apply_rope(a[..., Q_LORA_RANK + KV_LORA_RANK:], cos, sin)
    q = (c_q @ w_uq).reshape(B, S, N_HEADS, QK_NOPE_DIM + QK_ROPE_DIM)
    q_nope = q[..., :QK_NOPE_DIM]
    q_rope = _apply_rope(q[..., QK_NOPE_DIM:], cos[:, None, :], sin[:, None, :])
    kv = (c_kv @ w_ukv).reshape(B, S, N_HEADS, QK_NOPE_DIM + V_HEAD_DIM)
    k_nope, v = kv[..., :QK_NOPE_DIM], kv[..., QK_NOPE_DIM:]

    nb = S // Q_BLOCK

    def to_blocks(t):
        return jnp.moveaxis(t.reshape(B, nb, Q_BLOCK, *t.shape[2:]), 1, 0)

    def attend(blk):
        qn, qr = blk
        s = (jnp.einsum('bqhd,bkhd->bhqk', qn, k_nope, preferred_element_type=jnp.float32)
             + jnp.einsum('bqhd,bkd->bhqk', qr, k_rope, preferred_element_type=jnp.float32)) * ATTN_SCALE
        p = jax.nn.softmax(s, axis=-1).astype(v.dtype)
        return jnp.einsum('bhqk,bkhd->bqhd', p, v)

    o = lax.map(attend, (to_blocks(q_nope), to_blocks(q_rope)))
    o = jnp.moveaxis(o, 0, 1).reshape(B, S, N_HEADS * V_HEAD_DIM)
    return o @ w_o


def _conv_ffn(h, w_up, conv_w, conv_b, w_down):
    S = h.shape[1]
    u = h @ w_up
    pad = CONV_WIDTH // 2
    up = jnp.pad(u, ((0, 0), (pad, pad), (0, 0)))
    u = sum(up[:, k:k + S] * conv_w[k] for k in range(CONV_WIDTH)) + conv_b
    gate, val = u[..., :D_FF], u[..., D_FF:]
    return (jax.nn.silu(gate) * val) @ w_down


def _trunk(x, norm_mix, w_fourier_out, w_mla_in, g_mla_q, g_mla_kv, w_mla_uq, w_mla_ukv, w_mla_o,
           norm_ffn, w_ffn_up, conv_w, conv_b, w_ffn_down, norm_final):
    cos, sin = _rope_tables(x.shape[1])
    for i in range(DEPTH):
        h = _rmsnorm(x, norm_mix[i])
        j = i // N_MIXERS
        if i % N_MIXERS == 0:
            x = x + _fourier_mix(h, w_fourier_out[j])
        else:
            x = x + _mla(h, w_mla_in[j], g_mla_q[j], g_mla_kv[j], w_mla_uq[j], w_mla_ukv[j],
                         w_mla_o[j], cos, sin)
        h = _rmsnorm(x, norm_ffn[i])
        x = x + _conv_ffn(h, w_ffn_up[i], conv_w[i], conv_b[i], w_ffn_down[i])
    return _rmsnorm(x, norm_final)


def setup_inputs(seed: int = 0) -> dict:
    key = jax.random.key(seed)
    ks = jax.random.split(key, 18)
    f32 = jnp.float32

    def w(k, shape, fan_in):
        return jax.random.normal(k, shape, f32) * (fan_in ** -0.5)

    def gain(k, shape):
        return 1.0 + 0.01 * jax.random.normal(k, shape, f32)

    return {
        "x_prompt": jax.random.normal(ks[0], (BATCH, SEQ, D_MODEL), f32),
        "x_sample": jax.random.normal(ks[1], (DEC_BATCH, DEC_SEQ, D_MODEL), f32),
        "norm_mix": gain(ks[2], (DEPTH, D_MODEL)),
        "w_fourier_out": w(ks[3], (N_FOURIER_LAYERS, D_MODEL, D_MODEL), D_MODEL),
        "w_mla_in": w(ks[4], (N_MLA_LAYERS, D_MODEL, MLA_IN_DIM), D_MODEL),
        "g_mla_q": gain(ks[5], (N_MLA_LAYERS, Q_LORA_RANK)),
        "g_mla_kv": gain(ks[6], (N_MLA_LAYERS, KV_LORA_RANK)),
        "w_mla_uq": w(ks[7], (N_MLA_LAYERS, Q_LORA_RANK, N_HEADS * (QK_NOPE_DIM + QK_ROPE_DIM)), Q_LORA_RANK),
        "w_mla_ukv": w(ks[8], (N_MLA_LAYERS, KV_LORA_RANK, N_HEADS * (QK_NOPE_DIM + V_HEAD_DIM)), KV_LORA_RANK),
        "w_mla_o": w(ks[9], (N_MLA_LAYERS, N_HEADS * V_HEAD_DIM, D_MODEL), N_HEADS * V_HEAD_DIM),
        "norm_ffn": gain(ks[10], (DEPTH, D_MODEL)),
        "w_ffn_up": w(ks[11], (DEPTH, D_MODEL, 2 * D_FF), D_MODEL),
        "conv_w": w(ks[12], (DEPTH, CONV_WIDTH, 2 * D_FF), CONV_WIDTH),
        "conv_b": 0.01 * jax.random.normal(ks[13], (DEPTH, 2 * D_FF), f32),
        "w_ffn_down": w(ks[14], (DEPTH, D_FF, D_MODEL), D_FF),
        "norm_final": gain(ks[15], (D_MODEL,)),
    }


def reference(x_prompt, x_sample, norm_mix, w_fourier_out, w_mla_in, g_mla_q, g_mla_kv, w_mla_uq,
              w_mla_ukv, w_mla_o, norm_ffn, w_ffn_up, conv_w, conv_b, w_ffn_down, norm_final):
    y_prompt = _trunk(x_prompt, norm_mix, w_fourier_out, w_mla_in, g_mla_q, g_mla_kv, w_mla_uq,
                      w_mla_ukv, w_mla_o, norm_ffn, w_ffn_up, conv_w, conv_b, w_ffn_down, norm_final)
    y_sample = _trunk(x_sample, norm_mix, w_fourier_out, w_mla_in, g_mla_q, g_mla_kv, w_mla_uq,
                      w_mla_ukv, w_mla_o, norm_ffn, w_ffn_up, conv_w, conv_b, w_ffn_down, norm_final)
    return (y_prompt, y_sample)
```

```python
import functools
import math

import numpy as np
import jax
import jax.numpy as jnp
from jax import lax
from jax.experimental import pallas as pl
from jax.experimental.pallas import tpu as pltpu

N_HEADS = 16
QK_NOPE_DIM = 64
QK_ROPE_DIM = 32
V_HEAD_DIM = 64
N_FOURIER_GROUPS = 4
ROPE_THETA = 10000.0
CONV_WIDTH = 3
EPS = 1e-6

HEAD_LANES = 128
HALO_ROWS = 16
MIB = 1024 * 1024

BF16 = jnp.bfloat16
F32 = jnp.float32


def _rmsnorm(x, g):
    return x * lax.rsqrt(jnp.mean(x * x, axis=-1, keepdims=True) + EPS) * g


def _dot(a, b):
    return jnp.dot(a, b, preferred_element_type=F32)


def _params(semantics, vmem_mib):
    return pltpu.CompilerParams(dimension_semantics=semantics, vmem_limit_bytes=vmem_mib * MIB)


def _cdft_kernel(x_ref, g_ref, wc_ref, z1_ref, z2_ref):
    h = _rmsnorm(x_ref[0], g_ref[...]).astype(BF16)
    gd = wc_ref.shape[0]
    for g in range(N_FOURIER_GROUPS):
        zg = _dot(h[:, g * gd:(g + 1) * gd], wc_ref[...])
        z1_ref[0, :, g * gd:(g + 1) * gd] = zg[:, :gd].astype(BF16)
        z2_ref[0, :, g * gd:(g + 1) * gd] = zg[:, gd:].astype(BF16)


def _cdft(x, gain, wc):
    B, S, D = x.shape
    tm = min(S, 512)
    act = pl.BlockSpec((1, tm, D), lambda b, i: (b, i, 0))
    return pl.pallas_call(
        _cdft_kernel,
        grid=(B, S // tm),
        in_specs=[act, pl.BlockSpec((1, D), lambda b, i: (0, 0)),
                  pl.BlockSpec(wc.shape, lambda b, i: (0, 0))],
        out_specs=[act, act],
        out_shape=[jax.ShapeDtypeStruct((B, S, D), BF16)] * 2,
        compiler_params=_params(("parallel", "parallel"), 32),
        name="fourier_channel_dft",
    )(x, gain, wc)


def _sdft_kernel(cc_ref, cs_ref, z1_ref, z2_ref, x_ref, wo_ref, g_ref, xo_ref, h_ref):
    f = _dot(cc_ref[...], z1_ref[0]) + _dot(cs_ref[...], z2_ref[0])
    y = x_ref[0] + _dot(f.astype(BF16), wo_ref[...])
    xo_ref[0] = y
    h_ref[0] = _rmsnorm(y, g_ref[...]).astype(BF16)


def _sdft(cc, cs, z1, z2, x, wo, gain):
    B, S, D = x.shape
    tm = min(S, 256)
    act = pl.BlockSpec((1, tm, D), lambda b, i: (b, i, 0))
    seq = pl.BlockSpec((1, S, D), lambda b, i: (b, 0, 0))
    dft = pl.BlockSpec((tm, S), lambda b, i: (i, 0))
    return pl.pallas_call(
        _sdft_kernel,
        grid=(B, S // tm),
        in_specs=[dft, dft, seq, seq, act, pl.BlockSpec((D, D), lambda b, i: (0, 0)),
                  pl.BlockSpec((1, D), lambda b, i: (0, 0))],
        out_specs=[act, act],
        out_shape=[jax.ShapeDtypeStruct((B, S, D), F32), jax.ShapeDtypeStruct((B, S, D), BF16)],
        compiler_params=_params(("parallel", "parallel"), 48),
        name="fourier_position_dft_out",
    )(cc, cs, z1, z2, x, wo, gain)


def _ffn_kernel(hp_ref, hm_ref, hn_ref, x_ref, wg_ref, wv_ref, cwg_ref, cwv_ref, cbg_ref, cbv_ref,
                wd_ref, gn_ref, *rest, tm, ch, final):
    if final:
        y_ref, hh_ref, ug_ref, uv_ref, act_ref, acc_ref = rest
    else:
        xo_ref, ho_ref, hh_ref, ug_ref, uv_ref, act_ref, acc_ref = rest
    i, f = pl.program_id(1), pl.program_id(2)
    n_i, n_f = pl.num_programs(1), pl.num_programs(2)

    @pl.when(f == 0)
    def _():
        zeros = jnp.zeros((HALO_ROWS, hh_ref.shape[1]), BF16)
        hh_ref[0:HALO_ROWS] = jnp.where(i > 0, hp_ref[0], zeros)
        hh_ref[HALO_ROWS:HALO_ROWS + tm] = hm_ref[0]
        hh_ref[HALO_ROWS + tm:] = jnp.where(i < n_i - 1, hn_ref[0], zeros)
        acc_ref[...] = jnp.zeros_like(acc_ref)

    hh = hh_ref[...]
    ug_ref[...] = _dot(hh, wg_ref[...])
    uv_ref[...] = _dot(hh, wv_ref[...])
    cwg, cwv, cbg, cbv = cwg_ref[...], cwv_ref[...], cbg_ref[...], cbv_ref[...]

    def conv(u_ref, cw, cb, r):
        lo = u_ref[pl.ds(HALO_ROWS - 1 + r, ch), :]
        mid = u_ref[pl.ds(HALO_ROWS + r, ch), :]
        hi = u_ref[pl.ds(HALO_ROWS + 1 + r, ch), :]
        return lo * cw[0:1] + mid * cw[1:2] + hi * cw[2:3] + cb

    for r in range(0, tm, ch):
        gate = conv(ug_ref, cwg, cbg, r)
        val = conv(uv_ref, cwv, cbv, r)
        act_ref[pl.ds(r, ch), :] = (gate / (1.0 + jnp.exp(-gate)) * val).astype(BF16)
    acc_ref[...] += _dot(act_ref[...], wd_ref[...])

    @pl.when(f == n_f - 1)
    def _():
        y = x_ref[0] + acc_ref[...]
        if final:
            y_ref[0] = _rmsnorm(y, gn_ref[...])
        else:
            xo_ref[0] = y
            ho_ref[0] = _rmsnorm(y, gn_ref[...]).astype(BF16)


def _ffn(h, x, w_up, conv_w, conv_b, w_down, gain, final):
    B, S, D = x.shape
    d_ff = w_down.shape[0]
    tm = min(S, 1024)
    tf = 256
    ch = 32
    n_f = d_ff // tf
    hb = tm // HALO_ROWS
    last_hb = S // HALO_ROWS - 1
    act = pl.BlockSpec((1, tm, D), lambda b, i, f: (b, i, 0))
    in_specs = [
        pl.BlockSpec((1, HALO_ROWS, D), lambda b, i, f: (b, jnp.maximum(i * hb - 1, 0), 0)),
        act,
        pl.BlockSpec((1, HALO_ROWS, D), lambda b, i, f: (b, jnp.minimum((i + 1) * hb, last_hb), 0)),
        act,
        pl.BlockSpec((D, tf), lambda b, i, f: (0, f)),
        pl.BlockSpec((D, tf), lambda b, i, f: (0, n_f + f)),
        pl.BlockSpec((CONV_WIDTH, tf), lambda b, i, f: (0, f)),
        pl.BlockSpec((CONV_WIDTH, tf), lambda b, i, f: (0, n_f + f)),
        pl.BlockSpec((1, tf), lambda b, i, f: (0, f)),
        pl.BlockSpec((1, tf), lambda b, i, f: (0, n_f + f)),
        pl.BlockSpec((tf, D), lambda b, i, f: (f, 0)),
        pl.BlockSpec((1, D), lambda b, i, f: (0, 0)),
    ]
    if final:
        out_specs = act
        out_shape = jax.ShapeDtypeStruct((B, S, D), F32)
    else:
        out_specs = [act, act]
        out_shape = [jax.ShapeDtypeStruct((B, S, D), F32), jax.ShapeDtypeStruct((B, S, D), BF16)]
    rows = tm + 2 * HALO_ROWS
    return pl.pallas_call(
        functools.partial(_ffn_kernel, tm=tm, ch=ch, final=final),
        grid=(B, S // tm, n_f),
        in_specs=in_specs,
        out_specs=out_specs,
        out_shape=out_shape,
        scratch_shapes=[
            pltpu.VMEM((rows, D), BF16),
            pltpu.VMEM((rows, tf), F32),
            pltpu.VMEM((rows, tf), F32),
            pltpu.VMEM((tm, tf), BF16),
            pltpu.VMEM((tm, D), F32),
        ],
        compiler_params=_params(("parallel", "parallel", "arbitrary"), 52),
        name="conv_ffn_final" if final else "conv_ffn",
    )(h, h, h, x, w_up, w_up, conv_w, conv_w, conv_b, conv_b, w_down, gain)


def _mla_proj_kernel(h_ref, win_ref, gq_ref, gkv_ref, wqm_ref, wqs_ref, wk_ref, wv_ref,
                     cq_ref, sq_ref, ck_ref, sk_ref, q_ref, k_ref, v_ref, *, q_rank, kv_rank):
    a = _dot(h_ref[0], win_ref[...])
    c_q = _rmsnorm(a[:, :q_rank], gq_ref[...]).astype(BF16)
    r0 = q_rank + kv_rank
    c_kv = _rmsnorm(a[:, q_rank:r0], gkv_ref[...]).astype(BF16)
    k_rope = (a[:, r0:r0 + HEAD_LANES] * ck_ref[...] + a[:, r0 + HEAD_LANES:] * sk_ref[...]).astype(BF16)
    q_main = _dot(c_q, wqm_ref[...])
    q_swap = _dot(c_q, wqs_ref[...])
    cq, sq = cq_ref[...], sq_ref[...]
    for hd in range(N_HEADS):
        sl = slice(hd * HEAD_LANES, (hd + 1) * HEAD_LANES)
        q_ref[0, :, sl] = (q_main[:, sl] * cq + q_swap[:, sl] * sq).astype(BF16)
    k_ref[0] = _dot(jnp.concatenate([c_kv, k_rope], axis=1), wk_ref[...]).astype(BF16)
    v_ref[0] = _dot(c_kv, wv_ref[...]).astype(BF16)


def _mla_proj(h, p):
    B, S, D = h.shape
    tm = min(S, 256)
    hl = N_HEADS * HEAD_LANES
    hv = N_HEADS * V_HEAD_DIM
    const = lambda a: pl.BlockSpec(a.shape, lambda b, i: (0, 0))
    table = pl.BlockSpec((tm, HEAD_LANES), lambda b, i: (i, 0))
    weights = [p["w_in"], p["g_q"], p["g_kv"], p["w_q_main"], p["w_q_swap"], p["w_k"], p["w_v"]]
    return pl.pallas_call(
        functools.partial(_mla_proj_kernel, q_rank=p["g_q"].shape[1], kv_rank=p["g_kv"].shape[1]),
        grid=(B, S // tm),
        in_specs=[pl.BlockSpec((1, tm, D), lambda b, i: (b, i, 0))] + [const(w) for w in weights] + [table] * 4,
        out_specs=[pl.BlockSpec((1, tm, hl), lambda b, i: (b, i, 0)),
                   pl.BlockSpec((1, tm, hl), lambda b, i: (b, i, 0)),
                   pl.BlockSpec((1, tm, hv), lambda b, i: (b, i, 0))],
        out_shape=[jax.ShapeDtypeStruct((B, S, hl), BF16), jax.ShapeDtypeStruct((B, S, hl), BF16),
                   jax.ShapeDtypeStruct((B, S, hv), BF16)],
        compiler_params=_params(("parallel", "parallel"), 40),
        name="mla_projections",
    )(h, *weights, p["cos_q"], p["sin_q"], p["cos_k"], p["sin_k"])


def _attn_kernel(q_ref, k_ref, v_ref, o_ref):
    v = v_ref[0]
    outs = []
    for j in range(2):
        sl = slice(j * HEAD_LANES, (j + 1) * HEAD_LANES)
        s = lax.dot_general(q_ref[0, :, sl], k_ref[0, :, sl], (((1,), (1,)), ((), ())),
                            preferred_element_type=F32)
        p = jnp.exp2(s - jnp.max(s, axis=-1, keepdims=True))
        denom = jnp.sum(p, axis=-1, keepdims=True)
        outs.append(_dot(p.astype(BF16), v) / denom)
    lane = lax.broadcasted_iota(jnp.int32, outs[0].shape, 1)
    o_ref[0] = jnp.where(lane < V_HEAD_DIM, outs[0], outs[1]).astype(BF16)


def _attention(q, k, v):
    B, S, _ = q.shape
    tq = min(S, 256)
    return pl.pallas_call(
        _attn_kernel,
        grid=(B, N_HEADS // 2, S // tq),
        in_specs=[pl.BlockSpec((1, tq, 2 * HEAD_LANES), lambda b, hp, i: (b, i, hp)),
                  pl.BlockSpec((1, S, 2 * HEAD_LANES), lambda b, hp, i: (b, 0, hp)),
                  pl.BlockSpec((1, S, 2 * V_HEAD_DIM), lambda b, hp, i: (b, 0, hp))],
        out_specs=pl.BlockSpec((1, tq, 2 * V_HEAD_DIM), lambda b, hp, i: (b, i, hp)),
        out_shape=jax.ShapeDtypeStruct((B, S, N_HEADS * V_HEAD_DIM), BF16),
        compiler_params=_params(("parallel", "parallel", "parallel"), 40),
        name="mla_attention",
    )(q, k, v)


def _oproj_kernel(o_ref, x_ref, wo_ref, g_ref, xo_ref, h_ref):
    y = x_ref[0] + _dot(o_ref[0], wo_ref[...])
    xo_ref[0] = y
    h_ref[0] = _rmsnorm(y, g_ref[...]).astype(BF16)


def _oproj(o, x, wo, gain):
    B, S, D = x.shape
    tm = min(S, 512)
    act = pl.BlockSpec((1, tm, D), lambda b, i: (b, i, 0))
    return pl.pallas_call(
        _oproj_kernel,
        grid=(B, S // tm),
        in_specs=[pl.BlockSpec((1, tm, o.shape[2]), lambda b, i: (b, i, 0)), act,
                  pl.BlockSpec(wo.shape, lambda b, i: (0, 0)), pl.BlockSpec((1, D), lambda b, i: (0, 0))],
        out_specs=[act, act],
        out_shape=[jax.ShapeDtypeStruct((B, S, D), F32), jax.ShapeDtypeStruct((B, S, D), BF16)],
        compiler_params=_params(("parallel", "parallel"), 32),
        name="mla_out_projection",
    )(o, x, wo, gain)


def _dft_tables(S, gd):
    def cos_sin(n):
        idx = np.arange(n, dtype=np.int64)
        ang = 2.0 * np.pi * ((idx[:, None] * idx[None, :]) % n) / n
        return np.cos(ang) / math.sqrt(n), np.sin(ang) / math.sqrt(n)
    cc, sc = cos_sin(gd)
    cs, ss = cos_sin(S)
    as_bf16 = lambda a: jnp.asarray(a.astype(np.float32)).astype(BF16)
    return as_bf16(np.concatenate([cc, sc], axis=1)), as_bf16(cs), as_bf16(-ss)


def _prepare(S, norm_mix, w_fourier_out, w_mla_in, g_mla_q, g_mla_kv, w_mla_uq, w_mla_ukv, w_mla_o,
             norm_ffn, w_ffn_up, conv_w, conv_b, w_ffn_down, norm_final):
    D = norm_mix.shape[1]
    q_rank, kv_rank = g_mla_q.shape[1], g_mla_kv.shape[1]
    half = QK_ROPE_DIM // 2
    qk = QK_NOPE_DIM + QK_ROPE_DIM
    pad = HEAD_LANES - qk
    assert kv_rank == HEAD_LANES and q_rank % HEAD_LANES == 0 and D % (N_FOURIER_GROUPS * HEAD_LANES) == 0

    wc, cs_cos, cs_sin = _dft_tables(S, D // N_FOURIER_GROUPS)

    w_in = w_mla_in[0]
    r0 = q_rank + kv_rank
    rope_in = w_in[:, r0:]
    zeros = jnp.zeros((D, HEAD_LANES - QK_ROPE_DIM), F32)
    w_in_all = jnp.concatenate(
        [w_in[:, :r0], rope_in, zeros, -rope_in[:, half:], rope_in[:, :half], zeros], axis=1).astype(BF16)

    w_uq = w_mla_uq[0].reshape(q_rank, N_HEADS, qk)
    w_q_main = jnp.pad(w_uq, ((0, 0), (0, 0), (0, pad)))
    w_q_swap = jnp.concatenate(
        [jnp.zeros((q_rank, N_HEADS, QK_NOPE_DIM), F32), -w_uq[:, :, QK_NOPE_DIM + half:],
         w_uq[:, :, QK_NOPE_DIM:QK_NOPE_DIM + half], jnp.zeros((q_rank, N_HEADS, pad), F32)], axis=2)
    w_ukv = w_mla_ukv[0].reshape(kv_rank, N_HEADS, QK_NOPE_DIM + V_HEAD_DIM)
    w_k_nope = jnp.pad(w_ukv[:, :, :QK_NOPE_DIM], ((0, 0), (0, 0), (0, HEAD_LANES - QK_NOPE_DIM)))
    place = np.zeros((HEAD_LANES, N_HEADS, HEAD_LANES), np.float32)
    for j in range(QK_ROPE_DIM):
        place[j, :, QK_NOPE_DIM + j] = 1.0
    w_k = jnp.concatenate([w_k_nope, jnp.asarray(place)], axis=0)
    w_v = w_ukv[:, :, QK_NOPE_DIM:]
    flat = lambda w: w.reshape(w.shape[0], -1).astype(BF16)

    inv = 1.0 / (ROPE_THETA ** (jnp.arange(0, QK_ROPE_DIM, 2, dtype=F32) / QK_ROPE_DIM))
    ang = jnp.arange(S, dtype=F32)[:, None] * inv[None, :]
    cos, sin = jnp.cos(ang), jnp.sin(ang)
    q_scale = qk ** -0.5 * math.log2(math.e)
    ones = jnp.ones((S, QK_NOPE_DIM), F32)
    z_nope, z_pad = jnp.zeros((S, QK_NOPE_DIM), F32), jnp.zeros((S, pad), F32)
    z_k = jnp.zeros((S, HEAD_LANES - QK_ROPE_DIM), F32)

    return dict(
        wc=wc, cs_cos=cs_cos, cs_sin=cs_sin,
        norm_mix=norm_mix, norm_ffn=norm_ffn, norm_final=norm_final.reshape(1, D),
        w_fourier_out=w_fourier_out[0].astype(BF16),
        w_in=w_in_all, g_q=g_mla_q, g_kv=g_mla_kv,
        w_q_main=flat(w_q_main), w_q_swap=flat(w_q_swap), w_k=flat(w_k), w_v=flat(w_v),
        cos_q=jnp.concatenate([ones, cos, cos, z_pad], axis=1) * q_scale,
        sin_q=jnp.concatenate([z_nope, sin, sin, z_pad], axis=1) * q_scale,
        cos_k=jnp.concatenate([cos, cos, z_k], axis=1),
        sin_k=jnp.concatenate([sin, sin, z_k], axis=1),
        w_o=w_mla_o[0].astype(BF16),
        w_ffn_up=w_ffn_up.astype(BF16), conv_w=conv_w, conv_b=conv_b.reshape(conv_b.shape[0], 1, -1),
        w_ffn_down=w_ffn_down.astype(BF16),
    )


def _trunk(x, p):
    row = lambda g: g.reshape(1, -1)
    ffn = lambda h, x, i, gain, final: _ffn(h, x, p["w_ffn_up"][i], p["conv_w"][i], p["conv_b"][i],
                                             p["w_ffn_down"][i], gain, final)
    z1, z2 = _cdft(x, row(p["norm_mix"][0]), p["wc"])
    x, h = _sdft(p["cs_cos"], p["cs_sin"], z1, z2, x, p["w_fourier_out"], row(p["norm_ffn"][0]))
    x, h = ffn(h, x, 0, row(p["norm_mix"][1]), False)
    q, k, v = _mla_proj(h, p)
    o = _attention(q, k, v)
    x, h = _oproj(o, x, p["w_o"], row(p["norm_ffn"][1]))
    return ffn(h, x, 1, p["norm_final"], True)


def kernel(x_prompt, x_sample, norm_mix, w_fourier_out, w_mla_in, g_mla_q, g_mla_kv, w_mla_uq, w_mla_ukv,
           w_mla_o, norm_ffn, w_ffn_up, conv_w, conv_b, w_ffn_down, norm_final):
    assert norm_mix.shape[0] == 2 and x_prompt.shape[1:] == x_sample.shape[1:]
    p = _prepare(x_prompt.shape[1], norm_mix, w_fourier_out, w_mla_in, g_mla_q, g_mla_kv, w_mla_uq,
                 w_mla_ukv, w_mla_o, norm_ffn, w_ffn_up, conv_w, conv_b, w_ffn_down, norm_final)
    return (_trunk(x_prompt, p), _trunk(x_sample, p))
```

```python
import functools
import math

import numpy as np
import jax
import jax.numpy as jnp
from jax import lax
from jax.experimental import pallas as pl
from jax.experimental.pallas import tpu as pltpu

N_HEADS = 16
QK_NOPE_DIM = 64
QK_ROPE_DIM = 32
V_HEAD_DIM = 64
N_FOURIER_GROUPS = 4
ROPE_THETA = 10000.0
CONV_WIDTH = 3
EPS = 1e-6

HEAD_LANES = 128
HALO_ROWS = 16
FFN_TOKEN_TILE = 1024
FFN_FF_TILE = 256
FFN_ROW_BLOCK = 256
FFN_ROW_CHUNK = 32
ATTN_Q_TILE = 256
ATTN_HEADS_PER_STEP = 4

_NT_DIMS = (((1,), (1,)), ((), ()))
_TN_DIMS = (((0,), (0,)), ((), ()))
MIB = 1024 * 1024

BF16 = jnp.bfloat16
F32 = jnp.float32


def _rmsnorm(x, g):
    return x * lax.rsqrt(jnp.mean(x * x, axis=-1, keepdims=True) + EPS) * g


def _dot(a, b):
    return jnp.dot(a, b, preferred_element_type=F32)


def _params(semantics, vmem_mib, flags=None):
    return pltpu.CompilerParams(dimension_semantics=semantics, vmem_limit_bytes=vmem_mib * MIB, flags=flags)


def _cdft_kernel(x_ref, g_ref, wc_ref, z1_ref, z2_ref):
    h = _rmsnorm(x_ref[0], g_ref[...]).astype(BF16)
    gd = wc_ref.shape[0]
    for g in range(N_FOURIER_GROUPS):
        zg = _dot(h[:, g * gd:(g + 1) * gd], wc_ref[...])
        z1_ref[0, :, g * gd:(g + 1) * gd] = zg[:, :gd].astype(BF16)
        z2_ref[0, :, g * gd:(g + 1) * gd] = zg[:, gd:].astype(BF16)


def _cdft(x, gain, wc):
    B, S, D = x.shape
    tm = min(S, 512)
    act = pl.BlockSpec((1, tm, D), lambda b, i: (b, i, 0))
    return pl.pallas_call(
        _cdft_kernel,
        grid=(B, S // tm),
        in_specs=[act, pl.BlockSpec((1, D), lambda b, i: (0, 0)),
                  pl.BlockSpec(wc.shape, lambda b, i: (0, 0))],
        out_specs=[act, act],
        out_shape=[jax.ShapeDtypeStruct((B, S, D), BF16)] * 2,
        compiler_params=_params(("parallel", "parallel"), 32),
        name="fourier_channel_dft",
    )(x, gain, wc)


def _sdft_kernel(cc_ref, cs_ref, z1_ref, z2_ref, x_ref, wo_ref, g_ref, xo_ref, h_ref):
    f = _dot(cc_ref[...], z1_ref[0]) + _dot(cs_ref[...], z2_ref[0])
    y = x_ref[0] + _dot(f.astype(BF16), wo_ref[...])
    xo_ref[0] = y
    h_ref[0] = _rmsnorm(y, g_ref[...]).astype(BF16)


def _sdft(cc, cs, z1, z2, x, wo, gain):
    B, S, D = x.shape
    tm = min(S, 256)
    act = pl.BlockSpec((1, tm, D), lambda b, i: (b, i, 0))
    seq = pl.BlockSpec((1, S, D), lambda b, i: (b, 0, 0))
    dft = pl.BlockSpec((tm, S), lambda b, i: (i, 0))
    return pl.pallas_call(
        _sdft_kernel,
        grid=(B, S // tm),
        in_specs=[dft, dft, seq, seq, act, pl.BlockSpec((D, D), lambda b, i: (0, 0)),
                  pl.BlockSpec((1, D), lambda b, i: (0, 0))],
        out_specs=[act, act],
        out_shape=[jax.ShapeDtypeStruct((B, S, D), F32), jax.ShapeDtypeStruct((B, S, D), BF16)],
        compiler_params=_params(("parallel", "parallel"), 48),
        name="fourier_position_dft_out",
    )(cc, cs, z1, z2, x, wo, gain)


def _ffn_kernel(hp_ref, hm_ref, hn_ref, x_ref, wg_ref, wv_ref, cwg_ref, cwv_ref, cbg_ref, cbv_ref,
                wd_ref, gn_ref, *rest, tm, rb, ch, final):
    if final:
        y_ref, hh_ref, ug_ref, uv_ref, act_ref, acc_ref = rest
    else:
        xo_ref, ho_ref, hh_ref, ug_ref, uv_ref, act_ref, acc_ref = rest
    i, f = pl.program_id(1), pl.program_id(2)
    n_i, n_f = pl.num_programs(1), pl.num_programs(2)
    rows = tm + 2 * HALO_ROWS

    @pl.when(f == 0)
    def _():
        zeros = jnp.zeros((HALO_ROWS, hh_ref.shape[1]), BF16)
        hh_ref[0:HALO_ROWS] = jnp.where(i > 0, hp_ref[0], zeros)
        hh_ref[HALO_ROWS:HALO_ROWS + tm] = hm_ref[0]
        hh_ref[HALO_ROWS + tm:] = jnp.where(i < n_i - 1, hn_ref[0], zeros)
        acc_ref[...] = jnp.zeros_like(acc_ref)

    cwg, cwv, cbg, cbv = cwg_ref[...], cwv_ref[...], cbg_ref[...], cbv_ref[...]
    n_blocks = tm // rb
    bounds = [0] + [2 * HALO_ROWS + (k + 1) * rb for k in range(n_blocks - 1)] + [rows]

    def up(k):
        sl = slice(bounds[k], bounds[k + 1])
        hh = hh_ref[sl, :]
        ug_ref[sl, :] = _dot(hh, wg_ref[...])
        uv_ref[sl, :] = _dot(hh, wv_ref[...])

    def conv(u_ref, cw, cb, r):
        lo = u_ref[pl.ds(HALO_ROWS - 1 + r, ch), :]
        mid = u_ref[pl.ds(HALO_ROWS + r, ch), :]
        hi = u_ref[pl.ds(HALO_ROWS + 1 + r, ch), :]
        return lo * cw[0:1] + mid * cw[1:2] + hi * cw[2:3] + cb

    def down(k):
        for r in range(k * rb, (k + 1) * rb, ch):
            gate = conv(ug_ref, cwg, cbg, r)
            val = conv(uv_ref, cwv, cbv, r)
            act_ref[pl.ds(r, ch), :] = (gate / (1.0 + jnp.exp(-gate)) * val).astype(BF16)
        sl = slice(k * rb, (k + 1) * rb)
        acc_ref[sl, :] += _dot(act_ref[sl, :], wd_ref[...])

    up(0)
    for k in range(n_blocks):
        if k + 1 < n_blocks:
            up(k + 1)
        down(k)

    @pl.when(f == n_f - 1)
    def _():
        y = x_ref[0] + acc_ref[...]
        if final:
            y_ref[0] = _rmsnorm(y, gn_ref[...])
        else:
            xo_ref[0] = y
            ho_ref[0] = _rmsnorm(y, gn_ref[...]).astype(BF16)


def _ffn(h, x, w_up, conv_w, conv_b, w_down, gain, final):
    B, S, D = x.shape
    d_ff = w_down.shape[0]
    tm = min(S, FFN_TOKEN_TILE)
    rb = min(tm, FFN_ROW_BLOCK)
    tf = FFN_FF_TILE
    n_f = d_ff // tf
    hb = tm // HALO_ROWS
    last_hb = S // HALO_ROWS - 1
    act = pl.BlockSpec((1, tm, D), lambda b, i, f: (b, i, 0))
    in_specs = [
        pl.BlockSpec((1, HALO_ROWS, D), lambda b, i, f: (b, jnp.maximum(i * hb - 1, 0), 0)),
        act,
        pl.BlockSpec((1, HALO_ROWS, D), lambda b, i, f: (b, jnp.minimum((i + 1) * hb, last_hb), 0)),
        act,
        pl.BlockSpec((D, tf), lambda b, i, f: (0, f)),
        pl.BlockSpec((D, tf), lambda b, i, f: (0, n_f + f)),
        pl.BlockSpec((CONV_WIDTH, tf), lambda b, i, f: (0, f)),
        pl.BlockSpec((CONV_WIDTH, tf), lambda b, i, f: (0, n_f + f)),
        pl.BlockSpec((1, tf), lambda b, i, f: (0, f)),
        pl.BlockSpec((1, tf), lambda b, i, f: (0, n_f + f)),
        pl.BlockSpec((tf, D), lambda b, i, f: (f, 0)),
        pl.BlockSpec((1, D), lambda b, i, f: (0, 0)),
    ]
    if final:
        out_specs = act
        out_shape = jax.ShapeDtypeStruct((B, S, D), F32)
    else:
        out_specs = [act, act]
        out_shape = [jax.ShapeDtypeStruct((B, S, D), F32), jax.ShapeDtypeStruct((B, S, D), BF16)]
    rows = tm + 2 * HALO_ROWS
    return pl.pallas_call(
        functools.partial(_ffn_kernel, tm=tm, rb=rb, ch=FFN_ROW_CHUNK, final=final),
        grid=(B, S // tm, n_f),
        in_specs=in_specs,
        out_specs=out_specs,
        out_shape=out_shape,
        scratch_shapes=[
            pltpu.VMEM((rows, D), BF16),
            pltpu.VMEM((rows, tf), F32),
            pltpu.VMEM((rows, tf), F32),
            pltpu.VMEM((tm, tf), BF16),
            pltpu.VMEM((tm, D), F32),
        ],
        compiler_params=_params(("parallel", "parallel", "arbitrary"), 52),
        name="conv_ffn_final" if final else "conv_ffn",
    )(h, h, h, x, w_up, w_up, conv_w, conv_w, conv_b, conv_b, w_down, gain)


def _mla_proj_kernel(h_ref, win_ref, gq_ref, gkv_ref, wqm_ref, wqs_ref, wk_ref, wvt_ref,
                     cq_ref, sq_ref, ck_ref, sk_ref, q_ref, k_ref, vt_ref, *, q_rank, kv_rank):
    a = _dot(h_ref[0], win_ref[...])
    c_q = _rmsnorm(a[:, :q_rank], gq_ref[...]).astype(BF16)
    r0 = q_rank + kv_rank
    c_kv = _rmsnorm(a[:, q_rank:r0], gkv_ref[...]).astype(BF16)
    k_rope = (a[:, r0:r0 + HEAD_LANES] * ck_ref[...] + a[:, r0 + HEAD_LANES:] * sk_ref[...]).astype(BF16)
    q_main = _dot(c_q, wqm_ref[...])
    q_swap = _dot(c_q, wqs_ref[...])
    cq, sq = cq_ref[...], sq_ref[...]
    for hd in range(N_HEADS):
        sl = slice(hd * HEAD_LANES, (hd + 1) * HEAD_LANES)
        q_ref[0, :, sl] = (q_main[:, sl] * cq + q_swap[:, sl] * sq).astype(BF16)
    k_ref[0] = _dot(jnp.concatenate([c_kv, k_rope], axis=1), wk_ref[...]).astype(BF16)
    vt_ref[0] = lax.dot_general(wvt_ref[...], c_kv, _NT_DIMS, preferred_element_type=F32).astype(BF16)


def _mla_proj(h, p):
    B, S, D = h.shape
    tm = min(S, 256)
    hl = N_HEADS * HEAD_LANES
    hv = N_HEADS * V_HEAD_DIM
    const = lambda a: pl.BlockSpec(a.shape, lambda b, i: (0, 0))
    table = pl.BlockSpec((tm, HEAD_LANES), lambda b, i: (i, 0))
    weights = [p["w_in"], p["g_q"], p["g_kv"], p["w_q_main"], p["w_q_swap"], p["w_k"], p["w_v_t"]]
    return pl.pallas_call(
        functools.partial(_mla_proj_kernel, q_rank=p["g_q"].shape[1], kv_rank=p["g_kv"].shape[1]),
        grid=(B, S // tm),
        in_specs=[pl.BlockSpec((1, tm, D), lambda b, i: (b, i, 0))] + [const(w) for w in weights] + [table] * 4,
        out_specs=[pl.BlockSpec((1, tm, hl), lambda b, i: (b, i, 0)),
                   pl.BlockSpec((1, tm, hl), lambda b, i: (b, i, 0)),
                   pl.BlockSpec((1, hv, tm), lambda b, i: (b, 0, i))],
        out_shape=[jax.ShapeDtypeStruct((B, S, hl), BF16), jax.ShapeDtypeStruct((B, S, hl), BF16),
                   jax.ShapeDtypeStruct((B, hv, S), BF16)],
        compiler_params=_params(("parallel", "parallel"), 40),
        name="mla_projections",
    )(h, *weights, p["cos_q"], p["sin_q"], p["cos_k"], p["sin_k"])


def _attn_kernel(q_ref, k_ref, vt_ref, o_ref):
    n = ATTN_HEADS_PER_STEP

    def scores(j):
        sl = slice(j * HEAD_LANES, (j + 1) * HEAD_LANES)
        return lax.dot_general(k_ref[0, :, sl], q_ref[0, :, sl], _NT_DIMS,
                               preferred_element_type=F32)

    def finish(j, st):
        pt = jnp.exp2(st - jnp.max(st, axis=0, keepdims=True))
        denom = jnp.sum(pt, axis=0, keepdims=True)
        rows = slice(j * V_HEAD_DIM, (j + 1) * V_HEAD_DIM)
        ot = _dot(vt_ref[0, rows, :], pt.astype(BF16))
        o_ref[0, rows, :] = (ot / denom).astype(BF16)

    st = scores(0)
    for j in range(n):
        st_next = scores(j + 1) if j + 1 < n else None
        finish(j, st)
        st = st_next


def _attention(q, k, vt):
    B, S, _ = q.shape
    tq = min(S, ATTN_Q_TILE)
    n = ATTN_HEADS_PER_STEP
    return pl.pallas_call(
        _attn_kernel,
        grid=(B, N_HEADS // n, S // tq),
        in_specs=[pl.BlockSpec((1, tq, n * HEAD_LANES), lambda b, hp, i: (b, i, hp)),
                  pl.BlockSpec((1, S, n * HEAD_LANES), lambda b, hp, i: (b, 0, hp)),
                  pl.BlockSpec((1, n * V_HEAD_DIM, S), lambda b, hp, i: (b, hp, 0))],
        out_specs=pl.BlockSpec((1, n * V_HEAD_DIM, tq), lambda b, hp, i: (b, hp, i)),
        out_shape=jax.ShapeDtypeStruct((B, N_HEADS * V_HEAD_DIM, S), BF16),
        compiler_params=_params(("parallel", "parallel", "parallel"), 40),
        name="mla_attention",
    )(q, k, vt)


def _oproj_kernel(ot_ref, x_ref, wo_ref, g_ref, xo_ref, h_ref):
    y = x_ref[0] + lax.dot_general(ot_ref[0], wo_ref[...], _TN_DIMS, preferred_element_type=F32)
    xo_ref[0] = y
    h_ref[0] = _rmsnorm(y, g_ref[...]).astype(BF16)


def _oproj(ot, x, wo, gain):
    B, S, D = x.shape
    tm = min(S, 512)
    act = pl.BlockSpec((1, tm, D), lambda b, i: (b, i, 0))
    return pl.pallas_call(
        _oproj_kernel,
        grid=(B, S // tm),
        in_specs=[pl.BlockSpec((1, ot.shape[1], tm), lambda b, i: (b, 0, i)), act,
                  pl.BlockSpec(wo.shape, lambda b, i: (0, 0)), pl.BlockSpec((1, D), lambda b, i: (0, 0))],
        out_specs=[act, act],
        out_shape=[jax.ShapeDtypeStruct((B, S, D), F32), jax.ShapeDtypeStruct((B, S, D), BF16)],
        compiler_params=_params(("parallel", "parallel"), 32),
        name="mla_out_projection",
    )(ot, x, wo, gain)


def _dft_tables(S, gd):
    def cos_sin(n):
        idx = np.arange(n, dtype=np.int64)
        ang = 2.0 * np.pi * ((idx[:, None] * idx[None, :]) % n) / n
        return np.cos(ang) / math.sqrt(n), np.sin(ang) / math.sqrt(n)
    cc, sc = cos_sin(gd)
    cs, ss = cos_sin(S)
    as_bf16 = lambda a: jnp.asarray(a.astype(np.float32)).astype(BF16)
    return as_bf16(np.concatenate([cc, sc], axis=1)), as_bf16(cs), as_bf16(-ss)


def _prepare(S, norm_mix, w_fourier_out, w_mla_in, g_mla_q, g_mla_kv, w_mla_uq, w_mla_ukv, w_mla_o,
             norm_ffn, w_ffn_up, conv_w, conv_b, w_ffn_down, norm_final):
    D = norm_mix.shape[1]
    q_rank, kv_rank = g_mla_q.shape[1], g_mla_kv.shape[1]
    half = QK_ROPE_DIM // 2
    qk = QK_NOPE_DIM + QK_ROPE_DIM
    pad = HEAD_LANES - qk
    assert kv_rank == HEAD_LANES and q_rank % HEAD_LANES == 0 and D % (N_FOURIER_GROUPS * HEAD_LANES) == 0

    wc, cs_cos, cs_sin = _dft_tables(S, D // N_FOURIER_GROUPS)

    w_in = w_mla_in[0]
    r0 = q_rank + kv_rank
    rope_in = w_in[:, r0:]
    zeros = jnp.zeros((D, HEAD_LANES - QK_ROPE_DIM), F32)
    w_in_all = jnp.concatenate(
        [w_in[:, :r0], rope_in, zeros, -rope_in[:, half:], rope_in[:, :half], zeros], axis=1).astype(BF16)

    w_uq = w_mla_uq[0].reshape(q_rank, N_HEADS, qk)
    w_q_main = jnp.pad(w_uq, ((0, 0), (0, 0), (0, pad)))
    w_q_swap = jnp.concatenate(
        [jnp.zeros((q_rank, N_HEADS, QK_NOPE_DIM), F32), -w_uq[:, :, QK_NOPE_DIM + half:],
         w_uq[:, :, QK_NOPE_DIM:QK_NOPE_DIM + half], jnp.zeros((q_rank, N_HEADS, pad), F32)], axis=2)
    w_ukv = w_mla_ukv[0].reshape(kv_rank, N_HEADS, QK_NOPE_DIM + V_HEAD_DIM)
    w_k_nope = jnp.pad(w_ukv[:, :, :QK_NOPE_DIM], ((0, 0), (0, 0), (0, HEAD_LANES - QK_NOPE_DIM)))
    place = np.zeros((HEAD_LANES, N_HEADS, HEAD_LANES), np.float32)
    for j in range(QK_ROPE_DIM):
        place[j, :, QK_NOPE_DIM + j] = 1.0
    w_k = jnp.concatenate([w_k_nope, jnp.asarray(place)], axis=0)
    w_v = w_ukv[:, :, QK_NOPE_DIM:]
    flat = lambda w: w.reshape(w.shape[0], -1).astype(BF16)

    inv = 1.0 / (ROPE_THETA ** (jnp.arange(0, QK_ROPE_DIM, 2, dtype=F32) / QK_ROPE_DIM))
    ang = jnp.arange(S, dtype=F32)[:, None] * inv[None, :]
    cos, sin = jnp.cos(ang), jnp.sin(ang)
    q_scale = qk ** -0.5 * math.log2(math.e)
    ones = jnp.ones((S, QK_NOPE_DIM), F32)
    z_nope, z_pad = jnp.zeros((S, QK_NOPE_DIM), F32), jnp.zeros((S, pad), F32)
    z_k = jnp.zeros((S, HEAD_LANES - QK_ROPE_DIM), F32)

    return dict(
        wc=wc, cs_cos=cs_cos, cs_sin=cs_sin,
        norm_mix=norm_mix, norm_ffn=norm_ffn, norm_final=norm_final.reshape(1, D),
        w_fourier_out=w_fourier_out[0].astype(BF16),
        w_in=w_in_all, g_q=g_mla_q, g_kv=g_mla_kv,
        w_q_main=flat(w_q_main), w_q_swap=flat(w_q_swap), w_k=flat(w_k), w_v_t=flat(w_v).T,
        cos_q=jnp.concatenate([ones, cos, cos, z_pad], axis=1) * q_scale,
        sin_q=jnp.concatenate([z_nope, sin, sin, z_pad], axis=1) * q_scale,
        cos_k=jnp.concatenate([cos, cos, z_k], axis=1),
        sin_k=jnp.concatenate([sin, sin, z_k], axis=1),
        w_o=w_mla_o[0].astype(BF16),
        w_ffn_up=w_ffn_up.astype(BF16), conv_w=conv_w, conv_b=conv_b.reshape(conv_b.shape[0], 1, -1),
        w_ffn_down=w_ffn_down.astype(BF16),
    )


def _trunk(x, p):
    row = lambda g: g.reshape(1, -1)
    ffn = lambda h, x, i, gain, final: _ffn(h, x, p["w_ffn_up"][i], p["conv_w"][i], p["conv_b"][i],
                                             p["w_ffn_down"][i], gain, final)
    z1, z2 = _cdft(x, row(p["norm_mix"][0]), p["wc"])
    x, h = _sdft(p["cs_cos"], p["cs_sin"], z1, z2, x, p["w_fourier_out"], row(p["norm_ffn"][0]))
    x, h = ffn(h, x, 0, row(p["norm_mix"][1]), False)
    q, k, vt = _mla_proj(h, p)
    o = _attention(q, k, vt)
    x, h = _oproj(o, x, p["w_o"], row(p["norm_ffn"][1]))
    return ffn(h, x, 1, p["norm_final"], True)


def kernel(x_prompt, x_sample, norm_mix, w_fourier_out, w_mla_in, g_mla_q, g_mla_kv, w_mla_uq, w_mla_ukv,
           w_mla_o, norm_ffn, w_ffn_up, conv_w, conv_b, w_ffn_down, norm_final):
    assert norm_mix.shape[0] == 2 and x_prompt.shape[1:] == x_sample.shape[1:]
    p = _prepare(x_prompt.shape[1], norm_mix, w_fourier_out, w_mla_in, g_mla_q, g_mla_kv, w_mla_uq,
                 w_mla_ukv, w_mla_o, norm_ffn, w_ffn_up, conv_w, conv_b, w_ffn_down, norm_final)
    return (_trunk(x_prompt, p), _trunk(x_sample, p))
```

```python
import functools
import math

import numpy as np
import jax
import jax.numpy as jnp
from jax import lax
from jax.experimental import pallas as pl
from jax.experimental.pallas import tpu as pltpu

N_HEADS = 16
QK_NOPE_DIM = 64
QK_ROPE_DIM = 32
V_HEAD_DIM = 64
N_FOURIER_GROUPS = 4
ROPE_THETA = 10000.0
CONV_WIDTH = 3
EPS = 1e-6

HEAD_LANES = 128
HALO_ROWS = 16
FFN_TOKEN_TILE = 1024
FFN_FF_TILE = 256
FFN_ROW_BLOCK = 256
FFN_ROW_CHUNK = 32
ATTN_Q_TILE = 512
ATTN_HEADS_PER_STEP = 4

_NT_DIMS = (((1,), (1,)), ((), ()))
_TN_DIMS = (((0,), (0,)), ((), ()))
MIB = 1024 * 1024

BF16 = jnp.bfloat16
F32 = jnp.float32


def _rmsnorm(x, g):
    return x * lax.rsqrt(jnp.mean(x * x, axis=-1, keepdims=True) + EPS) * g


def _dot(a, b):
    return jnp.dot(a, b, preferred_element_type=F32)


def _params(semantics, vmem_mib, flags=None):
    return pltpu.CompilerParams(dimension_semantics=semantics, vmem_limit_bytes=vmem_mib * MIB, flags=flags)


def _cdft_kernel(x_ref, g_ref, wc_ref, z1_ref, z2_ref):
    h = _rmsnorm(x_ref[0], g_ref[...]).astype(BF16)
    gd = wc_ref.shape[0]
    for g in range(N_FOURIER_GROUPS):
        zg = _dot(h[:, g * gd:(g + 1) * gd], wc_ref[...])
        z1_ref[0, :, g * gd:(g + 1) * gd] = zg[:, :gd].astype(BF16)
        z2_ref[0, :, g * gd:(g + 1) * gd] = zg[:, gd:].astype(BF16)


def _cdft(x, gain, wc):
    B, S, D = x.shape
    tm = min(S, 512)
    act = pl.BlockSpec((1, tm, D), lambda b, i: (b, i, 0))
    return pl.pallas_call(
        _cdft_kernel,
        grid=(B, S // tm),
        in_specs=[act, pl.BlockSpec((1, D), lambda b, i: (0, 0)),
                  pl.BlockSpec(wc.shape, lambda b, i: (0, 0))],
        out_specs=[act, act],
        out_shape=[jax.ShapeDtypeStruct((B, S, D), BF16)] * 2,
        compiler_params=_params(("parallel", "parallel"), 32),
        name="fourier_channel_dft",
    )(x, gain, wc)


def _sdft_kernel(cc_ref, cs_ref, z1_ref, z2_ref, x_ref, wo_ref, g_ref, xo_ref, h_ref):
    f = _dot(cc_ref[...], z1_ref[0]) + _dot(cs_ref[...], z2_ref[0])
    y = x_ref[0] + _dot(f.astype(BF16), wo_ref[...])
    xo_ref[0] = y
    h_ref[0] = _rmsnorm(y, g_ref[...]).astype(BF16)


def _sdft(cc, cs, z1, z2, x, wo, gain):
    B, S, D = x.shape
    tm = min(S, 256)
    act = pl.BlockSpec((1, tm, D), lambda b, i: (b, i, 0))
    seq = pl.BlockSpec((1, S, D), lambda b, i: (b, 0, 0))
    dft = pl.BlockSpec((tm, S), lambda b, i: (i, 0))
    return pl.pallas_call(
        _sdft_kernel,
        grid=(B, S // tm),
        in_specs=[dft, dft, seq, seq, act, pl.BlockSpec((D, D), lambda b, i: (0, 0)),
                  pl.BlockSpec((1, D), lambda b, i: (0, 0))],
        out_specs=[act, act],
        out_shape=[jax.ShapeDtypeStruct((B, S, D), F32), jax.ShapeDtypeStruct((B, S, D), BF16)],
        compiler_params=_params(("parallel", "parallel"), 48),
        name="fourier_position_dft_out",
    )(cc, cs, z1, z2, x, wo, gain)


def _ffn_kernel(hp_ref, hm_ref, hn_ref, x_ref, wg_ref, wv_ref, cwg_ref, cwv_ref, cbg_ref, cbv_ref,
                wd_ref, gn_ref, *rest, tm, rb, ch, final):
    if final:
        y_ref, hh_ref, ug_ref, uv_ref, act_ref, acc_ref = rest
    else:
        xo_ref, ho_ref, hh_ref, ug_ref, uv_ref, act_ref, acc_ref = rest
    i, f = pl.program_id(1), pl.program_id(2)
    n_i, n_f = pl.num_programs(1), pl.num_programs(2)
    rows = tm + 2 * HALO_ROWS

    @pl.when(f == 0)
    def _():
        zeros = jnp.zeros((HALO_ROWS, hh_ref.shape[1]), BF16)
        hh_ref[0:HALO_ROWS] = jnp.where(i > 0, hp_ref[0], zeros)
        hh_ref[HALO_ROWS:HALO_ROWS + tm] = hm_ref[0]
        hh_ref[HALO_ROWS + tm:] = jnp.where(i < n_i - 1, hn_ref[0], zeros)
        acc_ref[...] = jnp.zeros_like(acc_ref)

    cwg, cwv, cbg, cbv = cwg_ref[...], cwv_ref[...], cbg_ref[...], cbv_ref[...]
    n_blocks = tm // rb
    bounds = [0] + [2 * HALO_ROWS + (k + 1) * rb for k in range(n_blocks - 1)] + [rows]

    def up(k):
        sl = slice(bounds[k], bounds[k + 1])
        hh = hh_ref[sl, :]
        ug_ref[sl, :] = _dot(hh, wg_ref[...])
        uv_ref[sl, :] = _dot(hh, wv_ref[...])

    def conv(u_ref, cw, cb, r):
        lo = u_ref[pl.ds(HALO_ROWS - 1 + r, ch), :]
        mid = u_ref[pl.ds(HALO_ROWS + r, ch), :]
        hi = u_ref[pl.ds(HALO_ROWS + 1 + r, ch), :]
        return lo * cw[0:1] + mid * cw[1:2] + hi * cw[2:3] + cb

    def down(k):
        for r in range(k * rb, (k + 1) * rb, ch):
            gate = conv(ug_ref, cwg, cbg, r)
            val = conv(uv_ref, cwv, cbv, r)
            act_ref[pl.ds(r, ch), :] = (gate / (1.0 + jnp.exp(-gate)) * val).astype(BF16)
        sl = slice(k * rb, (k + 1) * rb)
        acc_ref[sl, :] += _dot(act_ref[sl, :], wd_ref[...])

    up(0)
    for k in range(n_blocks):
        if k + 1 < n_blocks:
            up(k + 1)
        down(k)

    @pl.when(f == n_f - 1)
    def _():
        y = x_ref[0] + acc_ref[...]
        if final:
            y_ref[0] = _rmsnorm(y, gn_ref[...])
        else:
            xo_ref[0] = y
            ho_ref[0] = _rmsnorm(y, gn_ref[...]).astype(BF16)


def _ffn(h, x, w_up, conv_w, conv_b, w_down, gain, final):
    B, S, D = x.shape
    d_ff = w_down.shape[0]
    tm = min(S, FFN_TOKEN_TILE)
    rb = min(tm, FFN_ROW_BLOCK)
    tf = FFN_FF_TILE
    n_f = d_ff // tf
    hb = tm // HALO_ROWS
    last_hb = S // HALO_ROWS - 1
    act = pl.BlockSpec((1, tm, D), lambda b, i, f: (b, i, 0))
    in_specs = [
        pl.BlockSpec((1, HALO_ROWS, D), lambda b, i, f: (b, jnp.maximum(i * hb - 1, 0), 0)),
        act,
        pl.BlockSpec((1, HALO_ROWS, D), lambda b, i, f: (b, jnp.minimum((i + 1) * hb, last_hb), 0)),
        act,
        pl.BlockSpec((D, tf), lambda b, i, f: (0, f)),
        pl.BlockSpec((D, tf), lambda b, i, f: (0, n_f + f)),
        pl.BlockSpec((CONV_WIDTH, tf), lambda b, i, f: (0, f)),
        pl.BlockSpec((CONV_WIDTH, tf), lambda b, i, f: (0, n_f + f)),
        pl.BlockSpec((1, tf), lambda b, i, f: (0, f)),
        pl.BlockSpec((1, tf), lambda b, i, f: (0, n_f + f)),
        pl.BlockSpec((tf, D), lambda b, i, f: (f, 0)),
        pl.BlockSpec((1, D), lambda b, i, f: (0, 0)),
    ]
    if final:
        out_specs = act
        out_shape = jax.ShapeDtypeStruct((B, S, D), F32)
    else:
        out_specs = [act, act]
        out_shape = [jax.ShapeDtypeStruct((B, S, D), F32), jax.ShapeDtypeStruct((B, S, D), BF16)]
    rows = tm + 2 * HALO_ROWS
    return pl.pallas_call(
        functools.partial(_ffn_kernel, tm=tm, rb=rb, ch=FFN_ROW_CHUNK, final=final),
        grid=(B, S // tm, n_f),
        in_specs=in_specs,
        out_specs=out_specs,
        out_shape=out_shape,
        scratch_shapes=[
            pltpu.VMEM((rows, D), BF16),
            pltpu.VMEM((rows, tf), F32),
            pltpu.VMEM((rows, tf), F32),
            pltpu.VMEM((tm, tf), BF16),
            pltpu.VMEM((tm, D), F32),
        ],
        compiler_params=_params(("parallel", "parallel", "arbitrary"), 52),
        name="conv_ffn_final" if final else "conv_ffn",
    )(h, h, h, x, w_up, w_up, conv_w, conv_w, conv_b, conv_b, w_down, gain)


def _mla_proj_kernel(h_ref, win_ref, gq_ref, gkv_ref, wqm_ref, wqs_ref, wk_ref, wvt_ref,
                     cq_ref, sq_ref, ck_ref, sk_ref, q_ref, k_ref, vt_ref, *, q_rank, kv_rank):
    a = _dot(h_ref[0], win_ref[...])
    c_q = _rmsnorm(a[:, :q_rank], gq_ref[...]).astype(BF16)
    r0 = q_rank + kv_rank
    c_kv = _rmsnorm(a[:, q_rank:r0], gkv_ref[...]).astype(BF16)
    k_rope = (a[:, r0:r0 + HEAD_LANES] * ck_ref[...] + a[:, r0 + HEAD_LANES:] * sk_ref[...]).astype(BF16)
    q_main = _dot(c_q, wqm_ref[...])
    q_swap = _dot(c_q, wqs_ref[...])
    cq, sq = cq_ref[...], sq_ref[...]
    for hd in range(N_HEADS):
        sl = slice(hd * HEAD_LANES, (hd + 1) * HEAD_LANES)
        q_ref[0, :, sl] = (q_main[:, sl] * cq + q_swap[:, sl] * sq).astype(BF16)
    k_ref[0] = _dot(jnp.concatenate([c_kv, k_rope], axis=1), wk_ref[...]).astype(BF16)
    vt_ref[0] = lax.dot_general(wvt_ref[...], c_kv, _NT_DIMS, preferred_element_type=F32).astype(BF16)


def _mla_proj(h, p):
    B, S, D = h.shape
    tm = min(S, 256)
    hl = N_HEADS * HEAD_LANES
    hv = N_HEADS * V_HEAD_DIM
    const = lambda a: pl.BlockSpec(a.shape, lambda b, i: (0, 0))
    table = pl.BlockSpec((tm, HEAD_LANES), lambda b, i: (i, 0))
    weights = [p["w_in"], p["g_q"], p["g_kv"], p["w_q_main"], p["w_q_swap"], p["w_k"], p["w_v_t"]]
    return pl.pallas_call(
        functools.partial(_mla_proj_kernel, q_rank=p["g_q"].shape[1], kv_rank=p["g_kv"].shape[1]),
        grid=(B, S // tm),
        in_specs=[pl.BlockSpec((1, tm, D), lambda b, i: (b, i, 0))] + [const(w) for w in weights] + [table] * 4,
        out_specs=[pl.BlockSpec((1, tm, hl), lambda b, i: (b, i, 0)),
                   pl.BlockSpec((1, tm, hl), lambda b, i: (b, i, 0)),
                   pl.BlockSpec((1, hv, tm), lambda b, i: (b, 0, i))],
        out_shape=[jax.ShapeDtypeStruct((B, S, hl), BF16), jax.ShapeDtypeStruct((B, S, hl), BF16),
                   jax.ShapeDtypeStruct((B, hv, S), BF16)],
        compiler_params=_params(("parallel", "parallel"), 40),
        name="mla_projections",
    )(h, *weights, p["cos_q"], p["sin_q"], p["cos_k"], p["sin_k"])


def _attn_kernel(q_ref, k_ref, vt_ref, o_ref):
    n = ATTN_HEADS_PER_STEP

    def scores(j):
        sl = slice(j * HEAD_LANES, (j + 1) * HEAD_LANES)
        return lax.dot_general(k_ref[0, :, sl], q_ref[0, :, sl], _NT_DIMS,
                               preferred_element_type=F32)

    def finish(j, st):
        pt = jnp.exp2(st - jnp.max(st, axis=0, keepdims=True))
        denom = jnp.sum(pt, axis=0, keepdims=True)
        rows = slice(j * V_HEAD_DIM, (j + 1) * V_HEAD_DIM)
        ot = _dot(vt_ref[0, rows, :], pt.astype(BF16))
        o_ref[0, rows, :] = (ot / denom).astype(BF16)

    st = scores(0)
    for j in range(n):
        st_next = scores(j + 1) if j + 1 < n else None
        finish(j, st)
        st = st_next


def _attention(q, k, vt):
    B, S, _ = q.shape
    tq = min(S, ATTN_Q_TILE)
    n = ATTN_HEADS_PER_STEP
    return pl.pallas_call(
        _attn_kernel,
        grid=(B, N_HEADS // n, S // tq),
        in_specs=[pl.BlockSpec((1, tq, n * HEAD_LANES), lambda b, hp, i: (b, i, hp)),
                  pl.BlockSpec((1, S, n * HEAD_LANES), lambda b, hp, i: (b, 0, hp)),
                  pl.BlockSpec((1, n * V_HEAD_DIM, S), lambda b, hp, i: (b, hp, 0))],
        out_specs=pl.BlockSpec((1, n * V_HEAD_DIM, tq), lambda b, hp, i: (b, hp, i)),
        out_shape=jax.ShapeDtypeStruct((B, N_HEADS * V_HEAD_DIM, S), BF16),
        compiler_params=_params(("parallel", "parallel", "parallel"), 40),
        name="mla_attention",
    )(q, k, vt)


def _oproj_kernel(ot_ref, x_ref, wo_ref, g_ref, xo_ref, h_ref):
    y = x_ref[0] + lax.dot_general(ot_ref[0], wo_ref[...], _TN_DIMS, preferred_element_type=F32)
    xo_ref[0] = y
    h_ref[0] = _rmsnorm(y, g_ref[...]).astype(BF16)


def _oproj(ot, x, wo, gain):
    B, S, D = x.shape
    tm = min(S, 512)
    act = pl.BlockSpec((1, tm, D), lambda b, i: (b, i, 0))
    return pl.pallas_call(
        _oproj_kernel,
        grid=(B, S // tm),
        in_specs=[pl.BlockSpec((1, ot.shape[1], tm), lambda b, i: (b, 0, i)), act,
                  pl.BlockSpec(wo.shape, lambda b, i: (0, 0)), pl.BlockSpec((1, D), lambda b, i: (0, 0))],
        out_specs=[act, act],
        out_shape=[jax.ShapeDtypeStruct((B, S, D), F32), jax.ShapeDtypeStruct((B, S, D), BF16)],
        compiler_params=_params(("parallel", "parallel"), 32),
        name="mla_out_projection",
    )(ot, x, wo, gain)


def _dft_tables(S, gd):
    def cos_sin(n):
        idx = np.arange(n, dtype=np.int64)
        ang = 2.0 * np.pi * ((idx[:, None] * idx[None, :]) % n) / n
        return np.cos(ang) / math.sqrt(n), np.sin(ang) / math.sqrt(n)
    cc, sc = cos_sin(gd)
    cs, ss = cos_sin(S)
    as_bf16 = lambda a: jnp.asarray(a.astype(np.float32)).astype(BF16)
    return as_bf16(np.concatenate([cc, sc], axis=1)), as_bf16(cs), as_bf16(-ss)


def _prepare(S, norm_mix, w_fourier_out, w_mla_in, g_mla_q, g_mla_kv, w_mla_uq, w_mla_ukv, w_mla_o,
             norm_ffn, w_ffn_up, conv_w, conv_b, w_ffn_down, norm_final):
    D = norm_mix.shape[1]
    q_rank, kv_rank = g_mla_q.shape[1], g_mla_kv.shape[1]
    half = QK_ROPE_DIM // 2
    qk = QK_NOPE_DIM + QK_ROPE_DIM
    pad = HEAD_LANES - qk
    assert kv_rank == HEAD_LANES and q_rank % HEAD_LANES == 0 and D % (N_FOURIER_GROUPS * HEAD_LANES) == 0

    wc, cs_cos, cs_sin = _dft_tables(S, D // N_FOURIER_GROUPS)

    w_in = w_mla_in[0]
    r0 = q_rank + kv_rank
    rope_in = w_in[:, r0:]
    zeros = jnp.zeros((D, HEAD_LANES - QK_ROPE_DIM), F32)
    w_in_all = jnp.concatenate(
        [w_in[:, :r0], rope_in, zeros, -rope_in[:, half:], rope_in[:, :half], zeros], axis=1).astype(BF16)

    w_uq = w_mla_uq[0].reshape(q_rank, N_HEADS, qk)
    w_q_main = jnp.pad(w_uq, ((0, 0), (0, 0), (0, pad)))
    w_q_swap = jnp.concatenate(
        [jnp.zeros((q_rank, N_HEADS, QK_NOPE_DIM), F32), -w_uq[:, :, QK_NOPE_DIM + half:],
         w_uq[:, :, QK_NOPE_DIM:QK_NOPE_DIM + half], jnp.zeros((q_rank, N_HEADS, pad), F32)], axis=2)
    w_ukv = w_mla_ukv[0].reshape(kv_rank, N_HEADS, QK_NOPE_DIM + V_HEAD_DIM)
    w_k_nope = jnp.pad(w_ukv[:, :, :QK_NOPE_DIM], ((0, 0), (0, 0), (0, HEAD_LANES - QK_NOPE_DIM)))
    place = np.zeros((HEAD_LANES, N_HEADS, HEAD_LANES), np.float32)
    for j in range(QK_ROPE_DIM):
        place[j, :, QK_NOPE_DIM + j] = 1.0
    w_k = jnp.concatenate([w_k_nope, jnp.asarray(place)], axis=0)
    w_v = w_ukv[:, :, QK_NOPE_DIM:]
    flat = lambda w: w.reshape(w.shape[0], -1).astype(BF16)

    inv = 1.0 / (ROPE_THETA ** (jnp.arange(0, QK_ROPE_DIM, 2, dtype=F32) / QK_ROPE_DIM))
    ang = jnp.arange(S, dtype=F32)[:, None] * inv[None, :]
    cos, sin = jnp.cos(ang), jnp.sin(ang)
    q_scale = qk ** -0.5 * math.log2(math.e)
    ones = jnp.ones((S, QK_NOPE_DIM), F32)
    z_nope, z_pad = jnp.zeros((S, QK_NOPE_DIM), F32), jnp.zeros((S, pad), F32)
    z_k = jnp.zeros((S, HEAD_LANES - QK_ROPE_DIM), F32)

    return dict(
        wc=wc, cs_cos=cs_cos, cs_sin=cs_sin,
        norm_mix=norm_mix, norm_ffn=norm_ffn, norm_final=norm_final.reshape(1, D),
        w_fourier_out=w_fourier_out[0].astype(BF16),
        w_in=w_in_all, g_q=g_mla_q, g_kv=g_mla_kv,
        w_q_main=flat(w_q_main), w_q_swap=flat(w_q_swap), w_k=flat(w_k), w_v_t=flat(w_v).T,
        cos_q=jnp.concatenate([ones, cos, cos, z_pad], axis=1) * q_scale,
        sin_q=jnp.concatenate([z_nope, sin, sin, z_pad], axis=1) * q_scale,
        cos_k=jnp.concatenate([cos, cos, z_k], axis=1),
        sin_k=jnp.concatenate([sin, sin, z_k], axis=1),
        w_o=w_mla_o[0].astype(BF16),
        w_ffn_up=w_ffn_up.astype(BF16), conv_w=conv_w, conv_b=conv_b.reshape(conv_b.shape[0], 1, -1),
        w_ffn_down=w_ffn_down.astype(BF16),
    )


def _trunk(x, p):
    row = lambda g: g.reshape(1, -1)
    ffn = lambda h, x, i, gain, final: _ffn(h, x, p["w_ffn_up"][i], p["conv_w"][i], p["conv_b"][i],
                                             p["w_ffn_down"][i], gain, final)
    z1, z2 = _cdft(x, row(p["norm_mix"][0]), p["wc"])
    x, h = _sdft(p["cs_cos"], p["cs_sin"], z1, z2, x, p["w_fourier_out"], row(p["norm_ffn"][0]))
    x, h = ffn(h, x, 0, row(p["norm_mix"][1]), False)
    q, k, vt = _mla_proj(h, p)
    o = _attention(q, k, vt)
    x, h = _oproj(o, x, p["w_o"], row(p["norm_ffn"][1]))
    return ffn(h, x, 1, p["norm_final"], True)


def kernel(x_prompt, x_sample, norm_mix, w_fourier_out, w_mla_in, g_mla_q, g_mla_kv, w_mla_uq, w_mla_ukv,
           w_mla_o, norm_ffn, w_ffn_up, conv_w, conv_b, w_ffn_down, norm_final):
    assert norm_mix.shape[0] == 2 and x_prompt.shape[1:] == x_sample.shape[1:]
    p = _prepare(x_prompt.shape[1], norm_mix, w_fourier_out, w_mla_in, g_mla_q, g_mla_kv, w_mla_uq,
                 w_mla_ukv, w_mla_o, norm_ffn, w_ffn_up, conv_w, conv_b, w_ffn_down, norm_final)
    return (_trunk(x_prompt, p), _trunk(x_sample, p))
```

```python
import functools
import math

import numpy as np
import jax
import jax.numpy as jnp
from jax import lax
from jax.experimental import pallas as pl
from jax.experimental.pallas import tpu as pltpu

N_HEADS = 16
QK_NOPE_DIM = 64
QK_ROPE_DIM = 32
V_HEAD_DIM = 64
N_FOURIER_GROUPS = 4
ROPE_THETA = 10000.0
CONV_WIDTH = 3
EPS = 1e-6

HEAD_LANES = 128
SUBLANES = 8
LANES = 128
HALO_ROWS = 16
FFN_TOKEN_TILE = 1024
FFN_FF_TILE = 256
FFN_ROW_BLOCK = 256
FFN_ROW_CHUNK = 32
ATTN_Q_TILE = 512
ATTN_HEADS_PER_STEP = 4

_NT_DIMS = (((1,), (1,)), ((), ()))
_TN_DIMS = (((0,), (0,)), ((), ()))
MIB = 1024 * 1024

BF16 = jnp.bfloat16
F32 = jnp.float32


def _rmsnorm(x, g):
    return x * lax.rsqrt(jnp.mean(x * x, axis=-1, keepdims=True) + EPS) * g


def _store_interleaved(h, h_ref, tmp_ref, rb):
    g = rb // SUBLANES
    for c in range(h.shape[1] // LANES):
        lanes = slice(c * LANES, (c + 1) * LANES)
        for kb in range(h.shape[0] // rb):
            for q in range(SUBLANES):
                rows = slice(kb * rb + q * g, kb * rb + (q + 1) * g)
                tmp_ref[c, pl.ds(kb * rb + q, g, stride=SUBLANES), :] = h[rows, lanes]
        h_ref[0, :, lanes] = tmp_ref[c].astype(BF16)


def _dot(a, b):
    return jnp.dot(a, b, preferred_element_type=F32)


def _params(semantics, vmem_mib, flags=None):
    return pltpu.CompilerParams(dimension_semantics=semantics, vmem_limit_bytes=vmem_mib * MIB, flags=flags)


def _cdft_kernel(x_ref, g_ref, wc_ref, z1_ref, z2_ref):
    h = _rmsnorm(x_ref[0], g_ref[...]).astype(BF16)
    gd = wc_ref.shape[0]
    for g in range(N_FOURIER_GROUPS):
        zg = _dot(h[:, g * gd:(g + 1) * gd], wc_ref[...])
        z1_ref[0, :, g * gd:(g + 1) * gd] = zg[:, :gd].astype(BF16)
        z2_ref[0, :, g * gd:(g + 1) * gd] = zg[:, gd:].astype(BF16)


def _cdft(x, gain, wc):
    B, S, D = x.shape
    tm = min(S, 512)
    act = pl.BlockSpec((1, tm, D), lambda b, i: (b, i, 0))
    return pl.pallas_call(
        _cdft_kernel,
        grid=(B, S // tm),
        in_specs=[act, pl.BlockSpec((1, D), lambda b, i: (0, 0)),
                  pl.BlockSpec(wc.shape, lambda b, i: (0, 0))],
        out_specs=[act, act],
        out_shape=[jax.ShapeDtypeStruct((B, S, D), BF16)] * 2,
        compiler_params=_params(("parallel", "parallel"), 32),
        name="fourier_channel_dft",
    )(x, gain, wc)


def _sdft_kernel(cc_ref, cs_ref, z1_ref, z2_ref, x_ref, wo_ref, g_ref, xo_ref, h_ref, tmp_ref, *, rb):
    f = _dot(cc_ref[...], z1_ref[0]) + _dot(cs_ref[...], z2_ref[0])
    y = x_ref[0] + _dot(f.astype(BF16), wo_ref[...])
    xo_ref[0] = y
    _store_interleaved(_rmsnorm(y, g_ref[...]), h_ref, tmp_ref, rb)


def _sdft(cc, cs, z1, z2, x, wo, gain):
    B, S, D = x.shape
    tm = min(S, 256)
    act = pl.BlockSpec((1, tm, D), lambda b, i: (b, i, 0))
    seq = pl.BlockSpec((1, S, D), lambda b, i: (b, 0, 0))
    dft = pl.BlockSpec((tm, S), lambda b, i: (i, 0))
    return pl.pallas_call(
        functools.partial(_sdft_kernel, rb=min(tm, FFN_ROW_BLOCK)),
        grid=(B, S // tm),
        in_specs=[dft, dft, seq, seq, act, pl.BlockSpec((D, D), lambda b, i: (0, 0)),
                  pl.BlockSpec((1, D), lambda b, i: (0, 0))],
        out_specs=[act, act],
        out_shape=[jax.ShapeDtypeStruct((B, S, D), F32), jax.ShapeDtypeStruct((B, S, D), BF16)],
        scratch_shapes=[pltpu.VMEM((D // LANES, tm, LANES), F32)],
        compiler_params=_params(("parallel", "parallel"), 48),
        name="fourier_position_dft_out",
    )(cc, cs, z1, z2, x, wo, gain)


def _ffn_kernel(hp_ref, hm_ref, hn_ref, x_ref, wg_ref, wv_ref, cwg_ref, cwv_ref, cbg_ref, cbv_ref,
                wd_ref, gn_ref, *rest, tm, rb, ch, final):
    if final:
        y_ref, hh_ref, ug_ref, uv_ref, act_ref, acc_ref = rest
    else:
        xo_ref, ho_ref, hh_ref, ug_ref, uv_ref, act_ref, acc_ref = rest
    i, f = pl.program_id(1), pl.program_id(2)
    n_i, n_f = pl.num_programs(1), pl.num_programs(2)
    rows = tm + 2 * HALO_ROWS

    @pl.when(f == 0)
    def _():
        zeros = jnp.zeros((HALO_ROWS, hh_ref.shape[1]), BF16)
        hh_ref[0:HALO_ROWS] = jnp.where(i > 0, hp_ref[0], zeros)
        hh_ref[HALO_ROWS:HALO_ROWS + tm] = hm_ref[0]
        hh_ref[HALO_ROWS + tm:] = jnp.where(i < n_i - 1, hn_ref[0], zeros)
        acc_ref[...] = jnp.zeros_like(acc_ref)

    cwg, cwv, cbg, cbv = cwg_ref[...], cwv_ref[...], cbg_ref[...], cbv_ref[...]
    n_blocks = tm // rb
    bounds = [0] + [2 * HALO_ROWS + (k + 1) * rb for k in range(n_blocks - 1)] + [rows]

    def up(k):
        sl = slice(bounds[k], bounds[k + 1])
        hh = hh_ref[sl, :]
        ug_ref[sl, :] = _dot(hh, wg_ref[...])
        uv_ref[sl, :] = _dot(hh, wv_ref[...])

    sub = lax.broadcasted_iota(jnp.int32, (SUBLANES, ug_ref.shape[1]), 0)

    def conv_block(u_ref, cw, cb, k):
        base = HALO_ROWS + k * rb
        wrap_lo = u_ref[pl.ds(base + rb - SUBLANES - 1, SUBLANES), :]
        edge_lo = pltpu.roll(u_ref[base - SUBLANES:base, :], 1, 0)
        first_lo = jnp.where(sub == 0, edge_lo, wrap_lo)
        wrap_hi = u_ref[pl.ds(base + 1, SUBLANES), :]
        edge_hi = pltpu.roll(u_ref[base + rb:base + rb + SUBLANES, :], SUBLANES - 1, 0)
        last_hi = jnp.where(sub == SUBLANES - 1, edge_hi, wrap_hi)
        for r in range(0, rb, ch):
            mid = u_ref[base + r:base + r + ch, :]
            if r == 0:
                lo = jnp.concatenate([first_lo, u_ref[base:base + ch - SUBLANES, :]], axis=0)
            else:
                lo = u_ref[base + r - SUBLANES:base + r + ch - SUBLANES, :]
            if r + ch == rb:
                hi = jnp.concatenate([u_ref[base + r + SUBLANES:base + rb, :], last_hi], axis=0)
            else:
                hi = u_ref[base + r + SUBLANES:base + r + ch + SUBLANES, :]
            yield r, lo * cw[0:1] + mid * cw[1:2] + hi * cw[2:3] + cb

    def down(k):
        for (r, gate), (_, val) in zip(conv_block(ug_ref, cwg, cbg, k), conv_block(uv_ref, cwv, cbv, k)):
            act_ref[pl.ds(k * rb + r, ch), :] = (gate / (1.0 + jnp.exp(-gate)) * val).astype(BF16)
        sl = slice(k * rb, (k + 1) * rb)
        part = _dot(act_ref[sl, :], wd_ref[...])
        for c in range(acc_ref.shape[0]):
            acc_ref[c, sl, :] += part[:, c * LANES:(c + 1) * LANES]

    up(0)
    for k in range(n_blocks):
        if k + 1 < n_blocks:
            up(k + 1)
        down(k)

    @pl.when(f == n_f - 1)
    def _():
        g = rb // SUBLANES
        for kb in range(n_blocks):
            for q in range(SUBLANES):
                sl = slice(kb * rb + q * g, kb * rb + (q + 1) * g)
                rows = pl.ds(kb * rb + q, g, stride=SUBLANES)
                y = x_ref[0, sl, :] + jnp.concatenate(
                    [acc_ref[c, rows, :] for c in range(acc_ref.shape[0])], axis=1)
                if final:
                    y_ref[0, sl, :] = _rmsnorm(y, gn_ref[...])
                else:
                    xo_ref[0, sl, :] = y
                    ho_ref[0, sl, :] = _rmsnorm(y, gn_ref[...]).astype(BF16)


def _ffn(h, x, w_up, conv_w, conv_b, w_down, gain, final):
    B, S, D = x.shape
    d_ff = w_down.shape[0]
    tm = min(S, FFN_TOKEN_TILE)
    rb = min(tm, FFN_ROW_BLOCK)
    tf = FFN_FF_TILE
    n_f = d_ff // tf
    hb = tm // HALO_ROWS
    last_hb = S // HALO_ROWS - 1
    act = pl.BlockSpec((1, tm, D), lambda b, i, f: (b, i, 0))
    in_specs = [
        pl.BlockSpec((1, HALO_ROWS, D), lambda b, i, f: (b, jnp.maximum(i * hb - 1, 0), 0)),
        act,
        pl.BlockSpec((1, HALO_ROWS, D), lambda b, i, f: (b, jnp.minimum((i + 1) * hb, last_hb), 0)),
        act,
        pl.BlockSpec((D, tf), lambda b, i, f: (0, f)),
        pl.BlockSpec((D, tf), lambda b, i, f: (0, n_f + f)),
        pl.BlockSpec((CONV_WIDTH, tf), lambda b, i, f: (0, f)),
        pl.BlockSpec((CONV_WIDTH, tf), lambda b, i, f: (0, n_f + f)),
        pl.BlockSpec((1, tf), lambda b, i, f: (0, f)),
        pl.BlockSpec((1, tf), lambda b, i, f: (0, n_f + f)),
        pl.BlockSpec((tf, D), lambda b, i, f: (f, 0)),
        pl.BlockSpec((1, D), lambda b, i, f: (0, 0)),
    ]
    if final:
        out_specs = act
        out_shape = jax.ShapeDtypeStruct((B, S, D), F32)
    else:
        out_specs = [act, act]
        out_shape = [jax.ShapeDtypeStruct((B, S, D), F32), jax.ShapeDtypeStruct((B, S, D), BF16)]
    rows = tm + 2 * HALO_ROWS
    return pl.pallas_call(
        functools.partial(_ffn_kernel, tm=tm, rb=rb, ch=FFN_ROW_CHUNK, final=final),
        grid=(B, S // tm, n_f),
        in_specs=in_specs,
        out_specs=out_specs,
        out_shape=out_shape,
        scratch_shapes=[
            pltpu.VMEM((rows, D), BF16),
            pltpu.VMEM((rows, tf), F32),
            pltpu.VMEM((rows, tf), F32),
            pltpu.VMEM((tm, tf), BF16),
            pltpu.VMEM((D // LANES, tm, LANES), F32),
        ],
        compiler_params=_params(("parallel", "parallel", "arbitrary"), 52),
        name="conv_ffn_final" if final else "conv_ffn",
    )(h, h, h, x, w_up, w_up, conv_w, conv_w, conv_b, conv_b, w_down, gain)


def _mla_proj_kernel(h_ref, win_ref, gq_ref, gkv_ref, wqm_ref, wqs_ref, wk_ref, wvt_ref,
                     cq_ref, sq_ref, ck_ref, sk_ref, q_ref, k_ref, vt_ref, *, q_rank, kv_rank):
    a = _dot(h_ref[0], win_ref[...])
    c_q = _rmsnorm(a[:, :q_rank], gq_ref[...]).astype(BF16)
    r0 = q_rank + kv_rank
    c_kv = _rmsnorm(a[:, q_rank:r0], gkv_ref[...]).astype(BF16)
    k_rope = (a[:, r0:r0 + HEAD_LANES] * ck_ref[...] + a[:, r0 + HEAD_LANES:] * sk_ref[...]).astype(BF16)
    q_main = _dot(c_q, wqm_ref[...])
    q_swap = _dot(c_q, wqs_ref[...])
    cq, sq = cq_ref[...], sq_ref[...]
    for hd in range(N_HEADS):
        sl = slice(hd * HEAD_LANES, (hd + 1) * HEAD_LANES)
        q_ref[0, :, sl] = (q_main[:, sl] * cq + q_swap[:, sl] * sq).astype(BF16)
    k_ref[0] = _dot(jnp.concatenate([c_kv, k_rope], axis=1), wk_ref[...]).astype(BF16)
    vt_ref[0] = lax.dot_general(wvt_ref[...], c_kv, _NT_DIMS, preferred_element_type=F32).astype(BF16)


def _mla_proj(h, p):
    B, S, D = h.shape
    tm = min(S, 256)
    hl = N_HEADS * HEAD_LANES
    hv = N_HEADS * V_HEAD_DIM
    const = lambda a: pl.BlockSpec(a.shape, lambda b, i: (0, 0))
    table = pl.BlockSpec((tm, HEAD_LANES), lambda b, i: (i, 0))
    weights = [p["w_in"], p["g_q"], p["g_kv"], p["w_q_main"], p["w_q_swap"], p["w_k"], p["w_v_t"]]
    return pl.pallas_call(
        functools.partial(_mla_proj_kernel, q_rank=p["g_q"].shape[1], kv_rank=p["g_kv"].shape[1]),
        grid=(B, S // tm),
        in_specs=[pl.BlockSpec((1, tm, D), lambda b, i: (b, i, 0))] + [const(w) for w in weights] + [table] * 4,
        out_specs=[pl.BlockSpec((1, tm, hl), lambda b, i: (b, i, 0)),
                   pl.BlockSpec((1, tm, hl), lambda b, i: (b, i, 0)),
                   pl.BlockSpec((1, hv, tm), lambda b, i: (b, 0, i))],
        out_shape=[jax.ShapeDtypeStruct((B, S, hl), BF16), jax.ShapeDtypeStruct((B, S, hl), BF16),
                   jax.ShapeDtypeStruct((B, hv, S), BF16)],
        compiler_params=_params(("parallel", "parallel"), 40),
        name="mla_projections",
    )(h, *weights, p["cos_q"], p["sin_q"], p["cos_k"], p["sin_k"])


def _attn_kernel(q_ref, k_ref, vt_ref, o_ref):
    n = ATTN_HEADS_PER_STEP

    def scores(j):
        sl = slice(j * HEAD_LANES, (j + 1) * HEAD_LANES)
        return lax.dot_general(k_ref[0, :, sl], q_ref[0, :, sl], _NT_DIMS,
                               preferred_element_type=F32)

    def finish(j, st):
        pt = jnp.exp2(st - jnp.max(st, axis=0, keepdims=True))
        denom = jnp.sum(pt, axis=0, keepdims=True)
        rows = slice(j * V_HEAD_DIM, (j + 1) * V_HEAD_DIM)
        ot = _dot(vt_ref[0, rows, :], pt.astype(BF16))
        o_ref[0, rows, :] = (ot / denom).astype(BF16)

    st = scores(0)
    for j in range(n):
        st_next = scores(j + 1) if j + 1 < n else None
        finish(j, st)
        st = st_next


def _attention(q, k, vt):
    B, S, _ = q.shape
    tq = min(S, ATTN_Q_TILE)
    n = ATTN_HEADS_PER_STEP
    return pl.pallas_call(
        _attn_kernel,
        grid=(B, N_HEADS // n, S // tq),
        in_specs=[pl.BlockSpec((1, tq, n * HEAD_LANES), lambda b, hp, i: (b, i, hp)),
                  pl.BlockSpec((1, S, n * HEAD_LANES), lambda b, hp, i: (b, 0, hp)),
                  pl.BlockSpec((1, n * V_HEAD_DIM, S), lambda b, hp, i: (b, hp, 0))],
        out_specs=pl.BlockSpec((1, n * V_HEAD_DIM, tq), lambda b, hp, i: (b, hp, i)),
        out_shape=jax.ShapeDtypeStruct((B, N_HEADS * V_HEAD_DIM, S), BF16),
        compiler_params=_params(("parallel", "parallel", "parallel"), 40),
        name="mla_attention",
    )(q, k, vt)


def _oproj_kernel(ot_ref, x_ref, wo_ref, g_ref, xo_ref, h_ref, tmp_ref, *, rb):
    y = x_ref[0] + lax.dot_general(ot_ref[0], wo_ref[...], _TN_DIMS, preferred_element_type=F32)
    xo_ref[0] = y
    _store_interleaved(_rmsnorm(y, g_ref[...]), h_ref, tmp_ref, rb)


def _oproj(ot, x, wo, gain):
    B, S, D = x.shape
    tm = min(S, 512)
    act = pl.BlockSpec((1, tm, D), lambda b, i: (b, i, 0))
    return pl.pallas_call(
        functools.partial(_oproj_kernel, rb=min(tm, FFN_ROW_BLOCK)),
        grid=(B, S // tm),
        in_specs=[pl.BlockSpec((1, ot.shape[1], tm), lambda b, i: (b, 0, i)), act,
                  pl.BlockSpec(wo.shape, lambda b, i: (0, 0)), pl.BlockSpec((1, D), lambda b, i: (0, 0))],
        out_specs=[act, act],
        out_shape=[jax.ShapeDtypeStruct((B, S, D), F32), jax.ShapeDtypeStruct((B, S, D), BF16)],
        scratch_shapes=[pltpu.VMEM((D // LANES, tm, LANES), F32)],
        compiler_params=_params(("parallel", "parallel"), 32),
        name="mla_out_projection",
    )(ot, x, wo, gain)


def _dft_tables(S, gd):
    def cos_sin(n):
        idx = np.arange(n, dtype=np.int64)
        ang = 2.0 * np.pi * ((idx[:, None] * idx[None, :]) % n) / n
        return np.cos(ang) / math.sqrt(n), np.sin(ang) / math.sqrt(n)
    cc, sc = cos_sin(gd)
    cs, ss = cos_sin(S)
    as_bf16 = lambda a: jnp.asarray(a.astype(np.float32)).astype(BF16)
    return as_bf16(np.concatenate([cc, sc], axis=1)), as_bf16(cs), as_bf16(-ss)


def _prepare(S, norm_mix, w_fourier_out, w_mla_in, g_mla_q, g_mla_kv, w_mla_uq, w_mla_ukv, w_mla_o,
             norm_ffn, w_ffn_up, conv_w, conv_b, w_ffn_down, norm_final):
    D = norm_mix.shape[1]
    q_rank, kv_rank = g_mla_q.shape[1], g_mla_kv.shape[1]
    half = QK_ROPE_DIM // 2
    qk = QK_NOPE_DIM + QK_ROPE_DIM
    pad = HEAD_LANES - qk
    assert kv_rank == HEAD_LANES and q_rank % HEAD_LANES == 0 and D % (N_FOURIER_GROUPS * HEAD_LANES) == 0

    wc, cs_cos, cs_sin = _dft_tables(S, D // N_FOURIER_GROUPS)

    w_in = w_mla_in[0]
    r0 = q_rank + kv_rank
    rope_in = w_in[:, r0:]
    zeros = jnp.zeros((D, HEAD_LANES - QK_ROPE_DIM), F32)
    w_in_all = jnp.concatenate(
        [w_in[:, :r0], rope_in, zeros, -rope_in[:, half:], rope_in[:, :half], zeros], axis=1).astype(BF16)

    w_uq = w_mla_uq[0].reshape(q_rank, N_HEADS, qk)
    w_q_main = jnp.pad(w_uq, ((0, 0), (0, 0), (0, pad)))
    w_q_swap = jnp.concatenate(
        [jnp.zeros((q_rank, N_HEADS, QK_NOPE_DIM), F32), -w_uq[:, :, QK_NOPE_DIM + half:],
         w_uq[:, :, QK_NOPE_DIM:QK_NOPE_DIM + half], jnp.zeros((q_rank, N_HEADS, pad), F32)], axis=2)
    w_ukv = w_mla_ukv[0].reshape(kv_rank, N_HEADS, QK_NOPE_DIM + V_HEAD_DIM)
    w_k_nope = jnp.pad(w_ukv[:, :, :QK_NOPE_DIM], ((0, 0), (0, 0), (0, HEAD_LANES - QK_NOPE_DIM)))
    place = np.zeros((HEAD_LANES, N_HEADS, HEAD_LANES), np.float32)
    for j in range(QK_ROPE_DIM):
        place[j, :, QK_NOPE_DIM + j] = 1.0
    w_k = jnp.concatenate([w_k_nope, jnp.asarray(place)], axis=0)
    w_v = w_ukv[:, :, QK_NOPE_DIM:]
    flat = lambda w: w.reshape(w.shape[0], -1).astype(BF16)

    inv = 1.0 / (ROPE_THETA ** (jnp.arange(0, QK_ROPE_DIM, 2, dtype=F32) / QK_ROPE_DIM))
    ang = jnp.arange(S, dtype=F32)[:, None] * inv[None, :]
    cos, sin = jnp.cos(ang), jnp.sin(ang)
    q_scale = qk ** -0.5 * math.log2(math.e)
    ones = jnp.ones((S, QK_NOPE_DIM), F32)
    z_nope, z_pad = jnp.zeros((S, QK_NOPE_DIM), F32), jnp.zeros((S, pad), F32)
    z_k = jnp.zeros((S, HEAD_LANES - QK_ROPE_DIM), F32)

    return dict(
        wc=wc, cs_cos=cs_cos, cs_sin=cs_sin,
        norm_mix=norm_mix, norm_ffn=norm_ffn, norm_final=norm_final.reshape(1, D),
        w_fourier_out=w_fourier_out[0].astype(BF16),
        w_in=w_in_all, g_q=g_mla_q, g_kv=g_mla_kv,
        w_q_main=flat(w_q_main), w_q_swap=flat(w_q_swap), w_k=flat(w_k), w_v_t=flat(w_v).T,
        cos_q=jnp.concatenate([ones, cos, cos, z_pad], axis=1) * q_scale,
        sin_q=jnp.concatenate([z_nope, sin, sin, z_pad], axis=1) * q_scale,
        cos_k=jnp.concatenate([cos, cos, z_k], axis=1),
        sin_k=jnp.concatenate([sin, sin, z_k], axis=1),
        w_o=w_mla_o[0].astype(BF16),
        w_ffn_up=w_ffn_up.astype(BF16), conv_w=conv_w, conv_b=conv_b.reshape(conv_b.shape[0], 1, -1),
        w_ffn_down=w_ffn_down.astype(BF16),
    )


def _trunk(x, p):
    row = lambda g: g.reshape(1, -1)
    ffn = lambda h, x, i, gain, final: _ffn(h, x, p["w_ffn_up"][i], p["conv_w"][i], p["conv_b"][i],
                                             p["w_ffn_down"][i], gain, final)
    z1, z2 = _cdft(x, row(p["norm_mix"][0]), p["wc"])
    x, h = _sdft(p["cs_cos"], p["cs_sin"], z1, z2, x, p["w_fourier_out"], row(p["norm_ffn"][0]))
    x, h = ffn(h, x, 0, row(p["norm_mix"][1]), False)
    q, k, vt = _mla_proj(h, p)
    o = _attention(q, k, vt)
    x, h = _oproj(o, x, p["w_o"], row(p["norm_ffn"][1]))
    return ffn(h, x, 1, p["norm_final"], True)


def kernel(x_prompt, x_sample, norm_mix, w_fourier_out, w_mla_in, g_mla_q, g_mla_kv, w_mla_uq, w_mla_ukv,
           w_mla_o, norm_ffn, w_ffn_up, conv_w, conv_b, w_ffn_down, norm_final):
    assert norm_mix.shape[0] == 2 and x_prompt.shape[1:] == x_sample.shape[1:]
    p = _prepare(x_prompt.shape[1], norm_mix, w_fourier_out, w_mla_in, g_mla_q, g_mla_kv, w_mla_uq,
                 w_mla_ukv, w_mla_o, norm_ffn, w_ffn_up, conv_w, conv_b, w_ffn_down, norm_final)
    return (_trunk(x_prompt, p), _trunk(x_sample, p))
```

```python
import functools
import math

import numpy as np
import jax
import jax.numpy as jnp
from jax import lax
from jax.experimental import pallas as pl
from jax.experimental.pallas import tpu as pltpu

N_HEADS = 16
QK_NOPE_DIM = 64
QK_ROPE_DIM = 32
V_HEAD_DIM = 64
N_FOURIER_GROUPS = 4
ROPE_THETA = 10000.0
CONV_WIDTH = 3
EPS = 1e-6

HEAD_LANES = 128
SUBLANES = 8
LANES = 128
HALO_ROWS = 16
FOURIER_FOLD_TILE = 256
FFN_TOKEN_TILE = 1024
FFN_FF_TILE = 256
FFN_ROW_BLOCK = 256
FFN_ROW_CHUNK = 32
ATTN_Q_TILE = 512
ATTN_HEADS_PER_STEP = 4

_NT_DIMS = (((1,), (1,)), ((), ()))
_TN_DIMS = (((0,), (0,)), ((), ()))
MIB = 1024 * 1024

BF16 = jnp.bfloat16
F32 = jnp.float32


def _rmsnorm(x, g):
    return x * lax.rsqrt(jnp.mean(x * x, axis=-1, keepdims=True) + EPS) * g


def _store_interleaved(h, h_ref, tmp_ref, rb):
    g = rb // SUBLANES
    for c in range(h.shape[1] // LANES):
        lanes = slice(c * LANES, (c + 1) * LANES)
        for kb in range(h.shape[0] // rb):
            for q in range(SUBLANES):
                rows = slice(kb * rb + q * g, kb * rb + (q + 1) * g)
                tmp_ref[c, pl.ds(kb * rb + q, g, stride=SUBLANES), :] = h[rows, lanes]
        h_ref[0, :, lanes] = tmp_ref[c].astype(BF16)


def _dot(a, b):
    return jnp.dot(a, b, preferred_element_type=F32)


def _params(semantics, vmem_mib, flags=None):
    return pltpu.CompilerParams(dimension_semantics=semantics, vmem_limit_bytes=vmem_mib * MIB, flags=flags)


def _cdft_kernel(xa_ref, xb_ref, xc_ref, g_ref, wc_ref, flip_ref, e_ref, o_ref, zh_ref):
    i, n_i = pl.program_id(1), pl.num_programs(1)
    g = g_ref[...]
    ha = _rmsnorm(xa_ref[0], g)
    hb = _rmsnorm(xb_ref[0], g).astype(BF16)
    hc = _rmsnorm(xc_ref[0], g)
    h_rev = _dot(flip_ref[...], hb)
    first = jnp.where(i > 0, hc[0:1].astype(BF16).astype(F32), 0.0)
    sub = lax.broadcasted_iota(jnp.int32, (SUBLANES, h_rev.shape[1]), 0)
    h_rev = jnp.concatenate([jnp.where(sub == 0, first, h_rev[:SUBLANES]), h_rev[SUBLANES:]], axis=0)
    he = (ha + h_rev).astype(BF16)
    ho = (ha - h_rev).astype(BF16)
    gd = wc_ref.shape[0]
    for gi in range(N_FOURIER_GROUPS):
        sl = slice(gi * gd, (gi + 1) * gd)
        e_ref[0, :, sl] = _dot(he[:, sl], wc_ref[:, :gd]).astype(BF16)
        o_ref[0, :, sl] = _dot(ho[:, sl], wc_ref[:, gd:]).astype(BF16)

    @pl.when(i == n_i - 1)
    def _():
        for gi in range(N_FOURIER_GROUPS):
            sl = slice(gi * gd, (gi + 1) * gd)
            zh_ref[0, :, sl] = _dot(hb[:HALO_ROWS, sl], wc_ref[:, :gd])


def _cdft(x, gain, wc, flip):
    B, S, D = x.shape
    half = S // 2
    tm = flip.shape[0]
    n = S // tm
    last8 = S // SUBLANES - 1
    out = pl.BlockSpec((1, tm, D), lambda b, i: (b, i, 0))
    return pl.pallas_call(
        _cdft_kernel,
        grid=(B, half // tm),
        in_specs=[pl.BlockSpec((1, tm, D), lambda b, i: (b, i, 0)),
                  pl.BlockSpec((1, tm, D), lambda b, i: (b, n - 1 - i, 0)),
                  pl.BlockSpec((1, SUBLANES, D),
                               lambda b, i: (b, jnp.minimum((n - i) * (tm // SUBLANES), last8), 0)),
                  pl.BlockSpec((1, D), lambda b, i: (0, 0)),
                  pl.BlockSpec(wc.shape, lambda b, i: (0, 0)),
                  pl.BlockSpec(flip.shape, lambda b, i: (0, 0))],
        out_specs=[out, out, pl.BlockSpec((1, HALO_ROWS, D), lambda b, i: (b, 0, 0))],
        out_shape=[jax.ShapeDtypeStruct((B, half, D), BF16)] * 2 + [jax.ShapeDtypeStruct((B, HALO_ROWS, D), F32)],
        compiler_params=_params(("parallel", "arbitrary"), 32),
        name="fourier_channel_dft",
    )(x, x, x, gain, wc, flip)


def _sdft_kernel(cc_ref, cs_ref, e_ref, o_ref, zh_ref, x_ref, wo_ref, g_ref, xo_ref, h_ref, tmp_ref,
                 *, rb, inv_sqrt_n):
    f = _dot(cc_ref[...], e_ref[0]) + _dot(cs_ref[...], o_ref[0])
    row = lax.broadcasted_iota(jnp.int32, (f.shape[0], 1), 0)
    sign = (1 - 2 * (row & 1)).astype(F32)
    f = f + sign * (zh_ref[0, 0:1, :] * inv_sqrt_n)
    y = x_ref[0] + _dot(f.astype(BF16), wo_ref[...])
    xo_ref[0] = y
    _store_interleaved(_rmsnorm(y, g_ref[...]), h_ref, tmp_ref, rb)


def _sdft(cc, cs, e, o, zh, x, wo, gain):
    B, S, D = x.shape
    half = S // 2
    tm = min(S, 256)
    act = pl.BlockSpec((1, tm, D), lambda b, i: (b, i, 0))
    seq = pl.BlockSpec((1, half, D), lambda b, i: (b, 0, 0))
    dft = pl.BlockSpec((tm, half), lambda b, i: (i, 0))
    return pl.pallas_call(
        functools.partial(_sdft_kernel, rb=min(tm, FFN_ROW_BLOCK), inv_sqrt_n=S ** -0.5),
        grid=(B, S // tm),
        in_specs=[dft, dft, seq, seq, pl.BlockSpec((1, HALO_ROWS, D), lambda b, i: (b, 0, 0)), act,
                  pl.BlockSpec((D, D), lambda b, i: (0, 0)), pl.BlockSpec((1, D), lambda b, i: (0, 0))],
        out_specs=[act, act],
        out_shape=[jax.ShapeDtypeStruct((B, S, D), F32), jax.ShapeDtypeStruct((B, S, D), BF16)],
        scratch_shapes=[pltpu.VMEM((D // LANES, tm, LANES), F32)],
        compiler_params=_params(("parallel", "parallel"), 48),
        name="fourier_position_dft_out",
    )(cc, cs, e, o, zh, x, wo, gain)


def _ffn_kernel(hp_ref, hm_ref, hn_ref, x_ref, wg_ref, wv_ref, cwg_ref, cwv_ref, cbg_ref, cbv_ref,
                wd_ref, gn_ref, *rest, tm, rb, ch, final):
    if final:
        y_ref, hh_ref, ug_ref, uv_ref, act_ref, acc_ref = rest
    else:
        xo_ref, ho_ref, hh_ref, ug_ref, uv_ref, act_ref, acc_ref = rest
    i, f = pl.program_id(1), pl.program_id(2)
    n_i, n_f = pl.num_programs(1), pl.num_programs(2)
    rows = tm + 2 * HALO_ROWS

    @pl.when(f == 0)
    def _():
        zeros = jnp.zeros((HALO_ROWS, hh_ref.shape[1]), BF16)
        hh_ref[0:HALO_ROWS] = jnp.where(i > 0, hp_ref[0], zeros)
        hh_ref[HALO_ROWS:HALO_ROWS + tm] = hm_ref[0]
        hh_ref[HALO_ROWS + tm:] = jnp.where(i < n_i - 1, hn_ref[0], zeros)
        acc_ref[...] = jnp.zeros_like(acc_ref)

    cwg, cwv, cbg, cbv = cwg_ref[...], cwv_ref[...], cbg_ref[...], cbv_ref[...]
    n_blocks = tm // rb
    bounds = [0] + [2 * HALO_ROWS + (k + 1) * rb for k in range(n_blocks - 1)] + [rows]

    def up(k):
        sl = slice(bounds[k], bounds[k + 1])
        hh = hh_ref[sl, :]
        ug_ref[sl, :] = _dot(hh, wg_ref[...])
        uv_ref[sl, :] = _dot(hh, wv_ref[...])

    sub = lax.broadcasted_iota(jnp.int32, (SUBLANES, ug_ref.shape[1]), 0)

    def conv_block(u_ref, cw, cb, k):
        base = HALO_ROWS + k * rb
        wrap_lo = u_ref[pl.ds(base + rb - SUBLANES - 1, SUBLANES), :]
        edge_lo = pltpu.roll(u_ref[base - SUBLANES:base, :], 1, 0)
        first_lo = jnp.where(sub == 0, edge_lo, wrap_lo)
        wrap_hi = u_ref[pl.ds(base + 1, SUBLANES), :]
        edge_hi = pltpu.roll(u_ref[base + rb:base + rb + SUBLANES, :], SUBLANES - 1, 0)
        last_hi = jnp.where(sub == SUBLANES - 1, edge_hi, wrap_hi)
        for r in range(0, rb, ch):
            mid = u_ref[base + r:base + r + ch, :]
            if r == 0:
                lo = jnp.concatenate([first_lo, u_ref[base:base + ch - SUBLANES, :]], axis=0)
            else:
                lo = u_ref[base + r - SUBLANES:base + r + ch - SUBLANES, :]
            if r + ch == rb:
                hi = jnp.concatenate([u_ref[base + r + SUBLANES:base + rb, :], last_hi], axis=0)
            else:
                hi = u_ref[base + r + SUBLANES:base + r + ch + SUBLANES, :]
            yield r, lo * cw[0:1] + mid * cw[1:2] + hi * cw[2:3] + cb

    def down(k):
        for (r, gate), (_, val) in zip(conv_block(ug_ref, cwg, cbg, k), conv_block(uv_ref, cwv, cbv, k)):
            act_ref[pl.ds(k * rb + r, ch), :] = (gate / (1.0 + jnp.exp(-gate)) * val).astype(BF16)
        sl = slice(k * rb, (k + 1) * rb)
        part = _dot(act_ref[sl, :], wd_ref[...])
        for c in range(acc_ref.shape[0]):
            acc_ref[c, sl, :] += part[:, c * LANES:(c + 1) * LANES]

    up(0)
    for k in range(n_blocks):
        if k + 1 < n_blocks:
            up(k + 1)
        down(k)

    @pl.when(f == n_f - 1)
    def _():
        g = rb // SUBLANES
        for kb in range(n_blocks):
            for q in range(SUBLANES):
                sl = slice(kb * rb + q * g, kb * rb + (q + 1) * g)
                rows = pl.ds(kb * rb + q, g, stride=SUBLANES)
                y = x_ref[0, sl, :] + jnp.concatenate(
                    [acc_ref[c, rows, :] for c in range(acc_ref.shape[0])], axis=1)
                if final:
                    y_ref[0, sl, :] = _rmsnorm(y, gn_ref[...])
                else:
                    xo_ref[0, sl, :] = y
                    ho_ref[0, sl, :] = _rmsnorm(y, gn_ref[...]).astype(BF16)


def _ffn(h, x, w_up, conv_w, conv_b, w_down, gain, final):
    B, S, D = x.shape
    d_ff = w_down.shape[0]
    tm = min(S, FFN_TOKEN_TILE)
    rb = min(tm, FFN_ROW_BLOCK)
    tf = FFN_FF_TILE
    n_f = d_ff // tf
    hb = tm // HALO_ROWS
    last_hb = S // HALO_ROWS - 1
    act = pl.BlockSpec((1, tm, D), lambda b, i, f: (b, i, 0))
    in_specs = [
        pl.BlockSpec((1, HALO_ROWS, D), lambda b, i, f: (b, jnp.maximum(i * hb - 1, 0), 0)),
        act,
        pl.BlockSpec((1, HALO_ROWS, D), lambda b, i, f: (b, jnp.minimum((i + 1) * hb, last_hb), 0)),
        act,
        pl.BlockSpec((D, tf), lambda b, i, f: (0, f)),
        pl.BlockSpec((D, tf), lambda b, i, f: (0, n_f + f)),
        pl.BlockSpec((CONV_WIDTH, tf), lambda b, i, f: (0, f)),
        pl.BlockSpec((CONV_WIDTH, tf), lambda b, i, f: (0, n_f + f)),
        pl.BlockSpec((1, tf), lambda b, i, f: (0, f)),
        pl.BlockSpec((1, tf), lambda b, i, f: (0, n_f + f)),
        pl.BlockSpec((tf, D), lambda b, i, f: (f, 0)),
        pl.BlockSpec((1, D), lambda b, i, f: (0, 0)),
    ]
    if final:
        out_specs = act
        out_shape = jax.ShapeDtypeStruct((B, S, D), F32)
    else:
        out_specs = [act, act]
        out_shape = [jax.ShapeDtypeStruct((B, S, D), F32), jax.ShapeDtypeStruct((B, S, D), BF16)]
    rows = tm + 2 * HALO_ROWS
    return pl.pallas_call(
        functools.partial(_ffn_kernel, tm=tm, rb=rb, ch=FFN_ROW_CHUNK, final=final),
        grid=(B, S // tm, n_f),
        in_specs=in_specs,
        out_specs=out_specs,
        out_shape=out_shape,
        scratch_shapes=[
            pltpu.VMEM((rows, D), BF16),
            pltpu.VMEM((rows, tf), F32),
            pltpu.VMEM((rows, tf), F32),
            pltpu.VMEM((tm, tf), BF16),
            pltpu.VMEM((D // LANES, tm, LANES), F32),
        ],
        compiler_params=_params(("parallel", "parallel", "arbitrary"), 52),
        name="conv_ffn_final" if final else "conv_ffn",
    )(h, h, h, x, w_up, w_up, conv_w, conv_w, conv_b, conv_b, w_down, gain)


def _mla_proj_kernel(h_ref, win_ref, gq_ref, gkv_ref, wqm_ref, wqs_ref, wk_ref, wvt_ref,
                     cq_ref, sq_ref, ck_ref, sk_ref, q_ref, k_ref, vt_ref, *, q_rank, kv_rank):
    a = _dot(h_ref[0], win_ref[...])
    c_q = _rmsnorm(a[:, :q_rank], gq_ref[...]).astype(BF16)
    r0 = q_rank + kv_rank
    c_kv = _rmsnorm(a[:, q_rank:r0], gkv_ref[...]).astype(BF16)
    k_rope = (a[:, r0:r0 + HEAD_LANES] * ck_ref[...] + a[:, r0 + HEAD_LANES:] * sk_ref[...]).astype(BF16)
    q_main = _dot(c_q, wqm_ref[...])
    q_swap = _dot(c_q, wqs_ref[...])
    cq, sq = cq_ref[...], sq_ref[...]
    for hd in range(N_HEADS):
        sl = slice(hd * HEAD_LANES, (hd + 1) * HEAD_LANES)
        q_ref[0, :, sl] = (q_main[:, sl] * cq + q_swap[:, sl] * sq).astype(BF16)
    k_ref[0] = _dot(jnp.concatenate([c_kv, k_rope], axis=1), wk_ref[...]).astype(BF16)
    vt_ref[0] = lax.dot_general(wvt_ref[...], c_kv, _NT_DIMS, preferred_element_type=F32).astype(BF16)


def _mla_proj(h, p):
    B, S, D = h.shape
    tm = min(S, 256)
    hl = N_HEADS * HEAD_LANES
    hv = N_HEADS * V_HEAD_DIM
    const = lambda a: pl.BlockSpec(a.shape, lambda b, i: (0, 0))
    table = pl.BlockSpec((tm, HEAD_LANES), lambda b, i: (i, 0))
    weights = [p["w_in"], p["g_q"], p["g_kv"], p["w_q_main"], p["w_q_swap"], p["w_k"], p["w_v_t"]]
    return pl.pallas_call(
        functools.partial(_mla_proj_kernel, q_rank=p["g_q"].shape[1], kv_rank=p["g_kv"].shape[1]),
        grid=(B, S // tm),
        in_specs=[pl.BlockSpec((1, tm, D), lambda b, i: (b, i, 0))] + [const(w) for w in weights] + [table] * 4,
        out_specs=[pl.BlockSpec((1, tm, hl), lambda b, i: (b, i, 0)),
                   pl.BlockSpec((1, tm, hl), lambda b, i: (b, i, 0)),
                   pl.BlockSpec((1, hv, tm), lambda b, i: (b, 0, i))],
        out_shape=[jax.ShapeDtypeStruct((B, S, hl), BF16), jax.ShapeDtypeStruct((B, S, hl), BF16),
                   jax.ShapeDtypeStruct((B, hv, S), BF16)],
        compiler_params=_params(("parallel", "parallel"), 40),
        name="mla_projections",
    )(h, *weights, p["cos_q"], p["sin_q"], p["cos_k"], p["sin_k"])


def _attn_kernel(q_ref, k_ref, vt_ref, o_ref):
    n = ATTN_HEADS_PER_STEP

    def scores(j):
        sl = slice(j * HEAD_LANES, (j + 1) * HEAD_LANES)
        return lax.dot_general(k_ref[0, :, sl], q_ref[0, :, sl], _NT_DIMS,
                               preferred_element_type=F32)

    def softmax(st):
        pt = jnp.exp2(st - jnp.max(st, axis=0, keepdims=True))
        return pt.astype(BF16), jnp.sum(pt, axis=0, keepdims=True)

    def values(j, pt, denom):
        rows = slice(j * V_HEAD_DIM, (j + 1) * V_HEAD_DIM)
        o_ref[0, rows, :] = (_dot(vt_ref[0, rows, :], pt) / denom).astype(BF16)

    st = [scores(j) for j in range(min(2, n))]
    p = softmax(st[0])
    for j in range(n):
        if j + 2 < n:
            st.append(scores(j + 2))
        values(j, *p)
        if j + 1 < n:
            p = softmax(st[j + 1])


def _attention(q, k, vt):
    B, S, _ = q.shape
    tq = min(S, ATTN_Q_TILE)
    n = ATTN_HEADS_PER_STEP
    return pl.pallas_call(
        _attn_kernel,
        grid=(B, N_HEADS // n, S // tq),
        in_specs=[pl.BlockSpec((1, tq, n * HEAD_LANES), lambda b, hp, i: (b, i, hp)),
                  pl.BlockSpec((1, S, n * HEAD_LANES), lambda b, hp, i: (b, 0, hp)),
                  pl.BlockSpec((1, n * V_HEAD_DIM, S), lambda b, hp, i: (b, hp, 0))],
        out_specs=pl.BlockSpec((1, n * V_HEAD_DIM, tq), lambda b, hp, i: (b, hp, i)),
        out_shape=jax.ShapeDtypeStruct((B, N_HEADS * V_HEAD_DIM, S), BF16),
        compiler_params=_params(("parallel", "parallel", "parallel"), 40),
        name="mla_attention",
    )(q, k, vt)


def _oproj_kernel(ot_ref, x_ref, wo_ref, g_ref, xo_ref, h_ref, tmp_ref, *, rb):
    y = x_ref[0] + lax.dot_general(ot_ref[0], wo_ref[...], _TN_DIMS, preferred_element_type=F32)
    xo_ref[0] = y
    _store_interleaved(_rmsnorm(y, g_ref[...]), h_ref, tmp_ref, rb)


def _oproj(ot, x, wo, gain):
    B, S, D = x.shape
    tm = min(S, 512)
    act = pl.BlockSpec((1, tm, D), lambda b, i: (b, i, 0))
    return pl.pallas_call(
        functools.partial(_oproj_kernel, rb=min(tm, FFN_ROW_BLOCK)),
        grid=(B, S // tm),
        in_specs=[pl.BlockSpec((1, ot.shape[1], tm), lambda b, i: (b, 0, i)), act,
                  pl.BlockSpec(wo.shape, lambda b, i: (0, 0)), pl.BlockSpec((1, D), lambda b, i: (0, 0))],
        out_specs=[act, act],
        out_shape=[jax.ShapeDtypeStruct((B, S, D), F32), jax.ShapeDtypeStruct((B, S, D), BF16)],
        scratch_shapes=[pltpu.VMEM((D // LANES, tm, LANES), F32)],
        compiler_params=_params(("parallel", "parallel"), 32),
        name="mla_out_projection",
    )(ot, x, wo, gain)


def _dft_tables(S, gd, tm):
    def cos_sin(n):
        idx = np.arange(n, dtype=np.int64)
        ang = 2.0 * np.pi * ((idx[:, None] * idx[None, :]) % n) / n
        return np.cos(ang) / math.sqrt(n), np.sin(ang) / math.sqrt(n)
    cc, sc = cos_sin(gd)
    cs, ss = cos_sin(S)
    flip = np.zeros((tm, tm), np.float32)
    flip[np.arange(1, tm), tm - np.arange(1, tm)] = 1.0
    as_bf16 = lambda a: jnp.asarray(a.astype(np.float32)).astype(BF16)
    return (as_bf16(np.concatenate([cc, sc], axis=1)), as_bf16(cs[:, :S // 2]), as_bf16(-ss[:, :S // 2]),
            as_bf16(flip))


def _prepare(S, norm_mix, w_fourier_out, w_mla_in, g_mla_q, g_mla_kv, w_mla_uq, w_mla_ukv, w_mla_o,
             norm_ffn, w_ffn_up, conv_w, conv_b, w_ffn_down, norm_final):
    D = norm_mix.shape[1]
    q_rank, kv_rank = g_mla_q.shape[1], g_mla_kv.shape[1]
    half = QK_ROPE_DIM // 2
    qk = QK_NOPE_DIM + QK_ROPE_DIM
    pad = HEAD_LANES - qk
    assert kv_rank == HEAD_LANES and q_rank % HEAD_LANES == 0 and D % (N_FOURIER_GROUPS * HEAD_LANES) == 0

    wc, cs_cos, cs_sin, flip = _dft_tables(S, D // N_FOURIER_GROUPS, min(S // 2, FOURIER_FOLD_TILE))

    w_in = w_mla_in[0]
    r0 = q_rank + kv_rank
    rope_in = w_in[:, r0:]
    zeros = jnp.zeros((D, HEAD_LANES - QK_ROPE_DIM), F32)
    w_in_all = jnp.concatenate(
        [w_in[:, :r0], rope_in, zeros, -rope_in[:, half:], rope_in[:, :half], zeros], axis=1).astype(BF16)

    w_uq = w_mla_uq[0].reshape(q_rank, N_HEADS, qk)
    w_q_main = jnp.pad(w_uq, ((0, 0), (0, 0), (0, pad)))
    w_q_swap = jnp.concatenate(
        [jnp.zeros((q_rank, N_HEADS, QK_NOPE_DIM), F32), -w_uq[:, :, QK_NOPE_DIM + half:],
         w_uq[:, :, QK_NOPE_DIM:QK_NOPE_DIM + half], jnp.zeros((q_rank, N_HEADS, pad), F32)], axis=2)
    w_ukv = w_mla_ukv[0].reshape(kv_rank, N_HEADS, QK_NOPE_DIM + V_HEAD_DIM)
    w_k_nope = jnp.pad(w_ukv[:, :, :QK_NOPE_DIM], ((0, 0), (0, 0), (0, HEAD_LANES - QK_NOPE_DIM)))
    place = np.zeros((HEAD_LANES, N_HEADS, HEAD_LANES), np.float32)
    for j in range(QK_ROPE_DIM):
        place[j, :, QK_NOPE_DIM + j] = 1.0
    w_k = jnp.concatenate([w_k_nope, jnp.asarray(place)], axis=0)
    w_v = w_ukv[:, :, QK_NOPE_DIM:]
    flat = lambda w: w.reshape(w.shape[0], -1).astype(BF16)

    inv = 1.0 / (ROPE_THETA ** (jnp.arange(0, QK_ROPE_DIM, 2, dtype=F32) / QK_ROPE_DIM))
    ang = jnp.arange(S, dtype=F32)[:, None] * inv[None, :]
    cos, sin = jnp.cos(ang), jnp.sin(ang)
    q_scale = qk ** -0.5 * math.log2(math.e)
    ones = jnp.ones((S, QK_NOPE_DIM), F32)
    z_nope, z_pad = jnp.zeros((S, QK_NOPE_DIM), F32), jnp.zeros((S, pad), F32)
    z_k = jnp.zeros((S, HEAD_LANES - QK_ROPE_DIM), F32)

    return dict(
        wc=wc, cs_cos=cs_cos, cs_sin=cs_sin, flip=flip,
        norm_mix=norm_mix, norm_ffn=norm_ffn, norm_final=norm_final.reshape(1, D),
        w_fourier_out=w_fourier_out[0].astype(BF16),
        w_in=w_in_all, g_q=g_mla_q, g_kv=g_mla_kv,
        w_q_main=flat(w_q_main), w_q_swap=flat(w_q_swap), w_k=flat(w_k), w_v_t=flat(w_v).T,
        cos_q=jnp.concatenate([ones, cos, cos, z_pad], axis=1) * q_scale,
        sin_q=jnp.concatenate([z_nope, sin, sin, z_pad], axis=1) * q_scale,
        cos_k=jnp.concatenate([cos, cos, z_k], axis=1),
        sin_k=jnp.concatenate([sin, sin, z_k], axis=1),
        w_o=w_mla_o[0].astype(BF16),
        w_ffn_up=w_ffn_up.astype(BF16), conv_w=conv_w, conv_b=conv_b.reshape(conv_b.shape[0], 1, -1),
        w_ffn_down=w_ffn_down.astype(BF16),
    )


def _trunk(x, p):
    row = lambda g: g.reshape(1, -1)
    ffn = lambda h, x, i, gain, final: _ffn(h, x, p["w_ffn_up"][i], p["conv_w"][i], p["conv_b"][i],
                                             p["w_ffn_down"][i], gain, final)
    e, o, zh = _cdft(x, row(p["norm_mix"][0]), p["wc"], p["flip"])
    x, h = _sdft(p["cs_cos"], p["cs_sin"], e, o, zh, x, p["w_fourier_out"], row(p["norm_ffn"][0]))
    x, h = ffn(h, x, 0, row(p["norm_mix"][1]), False)
    q, k, vt = _mla_proj(h, p)
    o = _attention(q, k, vt)
    x, h = _oproj(o, x, p["w_o"], row(p["norm_ffn"][1]))
    return ffn(h, x, 1, p["norm_final"], True)


def kernel(x_prompt, x_sample, norm_mix, w_fourier_out, w_mla_in, g_mla_q, g_mla_kv, w_mla_uq, w_mla_ukv,
           w_mla_o, norm_ffn, w_ffn_up, conv_w, conv_b, w_ffn_down, norm_final):
    assert norm_mix.shape[0] == 2 and x_prompt.shape[1:] == x_sample.shape[1:]
    p = _prepare(x_prompt.shape[1], norm_mix, w_fourier_out, w_mla_in, g_mla_q, g_mla_kv, w_mla_uq,
                 w_mla_ukv, w_mla_o, norm_ffn, w_ffn_up, conv_w, conv_b, w_ffn_down, norm_final)
    return (_trunk(x_prompt, p), _trunk(x_sample, p))
```
